```python
import math, functools
import jax, jax.numpy as jnp
from jax import lax
import numpy as np

D_MODEL = 2048
BATCH = 8
SEQ = 2048
DEPTH = 2
DEC_BATCH = 128
DEC_SEQ = 1
PAST_LEN = 8192
PAGE_SIZE = 128

HEAD_DIM = 64
DILATIONS = ((128, 1), (512, 4), (2048, 16))
N_DIL = 3
A_HEADS = 8
B_HEADS = 16
B_KV_HEADS = 2
B_GROUP = B_HEADS // B_KV_HEADS
B_WINDOW = 128
ROPE_THETA = 150000.0
N_BUCKETS = 32
T5_MAX_DIST = 2048
D_FF = ((8 * D_MODEL + 3 * 256 - 1) // (3 * 256)) * 256
ALPHA = (2 * DEPTH) ** 0.25
BETA = (8 * DEPTH) ** -0.25
LN_EPS = 1e-5
NEG_INF = -1e30

A_COLS = N_DIL * 3 * A_HEADS * HEAD_DIM
B_Q = B_HEADS * HEAD_DIM
B_KV = B_KV_HEADS * HEAD_DIM
IN_COLS = A_COLS + B_Q + 2 * B_KV + 2 * D_MODEL

kernel_name = "hybrid_dilated_swa_sink_decoder_step"


def layer_norm(x, g, b):
    xf = x.astype(jnp.float32)
    mu = jnp.mean(xf, -1, keepdims=True)
    var = jnp.mean(jnp.square(xf - mu), -1, keepdims=True)
    return ((xf - mu) * lax.rsqrt(var + LN_EPS) * g + b).astype(x.dtype)


def t5_bucket(dist):
    exact = N_BUCKETS // 2
    n = jnp.maximum(dist, 0)
    log_ratio = jnp.log(jnp.maximum(n, exact).astype(jnp.float32) / exact) / math.log(T5_MAX_DIST / exact)
    large = jnp.minimum(exact + (log_ratio * (N_BUCKETS - exact)).astype(jnp.int32), N_BUCKETS - 1)
    return jnp.where(n < exact, n, large)


def dilated_bias(t5_table, g):
    w, d = DILATIONS[g]
    b = t5_table[t5_bucket(d * jnp.arange(w // d + 1)), g * A_HEADS:(g + 1) * A_HEADS]
    return b.T[:, None, :]


def rope(x, pos):
    half = HEAD_DIM // 2
    inv = ROPE_THETA ** (-jnp.arange(half, dtype=jnp.float32) / half)
    ang = pos.astype(jnp.float32)[:, None] * inv[None]
    ang = ang.reshape((ang.shape[0],) + (1,) * (x.ndim - 3) + (half,))
    cos, sin = jnp.cos(ang), jnp.sin(ang)
    x1 = x[..., :half].astype(jnp.float32)
    x2 = x[..., half:].astype(jnp.float32)
    return jnp.concatenate([x1 * cos - x2 * sin, x2 * cos + x1 * sin], -1).astype(x.dtype)


def masked_softmax(s, valid, sink):
    s = jnp.where(valid, s, NEG_INF)
    m = jnp.max(s, -1, keepdims=True)
    if sink is not None:
        sink = sink.astype(jnp.float32)
        m = jnp.maximum(m, sink)
    p = jnp.exp(s - m)
    denom = jnp.sum(p, -1, keepdims=True)
    if sink is not None:
        denom = denom + jnp.exp(sink - m)
    return p / denom, (m + jnp.log(denom))[..., 0]


def band_attention(q, k, v, window, bias, sink):
    Bn, N, Hkv, G, dh = q.shape
    blk = window
    nb = -(-N // blk)
    pad = nb * blk - N
    qb = jnp.pad(q, ((0, 0), (0, pad), (0, 0), (0, 0), (0, 0))).reshape(Bn, nb, blk, Hkv, G, dh)

    def kv_blocks(t):
        t = jnp.pad(t, ((0, 0), (blk, pad), (0, 0), (0, 0))).reshape(Bn, nb + 1, blk, Hkv, dh)
        return jnp.concatenate([t[:, :-1], t[:, 1:]], axis=2)

    kb, vb = kv_blocks(k), kv_blocks(v)
    s = jnp.einsum('bnqhgd,bnkhd->bnhgqk', qb, kb, preferred_element_type=jnp.float32) * (HEAD_DIM ** -0.5)
    dist = blk + jnp.arange(blk)[:, None] - jnp.arange(2 * blk)[None, :]
    kpos = (jnp.arange(nb)[:, None] - 1) * blk + jnp.arange(2 * blk)[None, :]
    valid = ((dist >= 0) & (dist <= window))[None] & (kpos >= 0)[:, None, :]
    if bias is not None:
        s = s + bias[:, :, jnp.clip(dist, 0, window)].astype(jnp.float32)
    p, lse = masked_softmax(s, valid[:, None, None], None if sink is None else sink[:, :, None, None])
    out = jnp.einsum('bnhgqk,bnkhd->bnqhgd', p, vb.astype(jnp.float32))
    out = out.reshape(Bn, nb * blk, Hkv, G, dh)[:, :N]
    lse = lse.transpose(0, 1, 4, 2, 3).reshape(Bn, nb * blk, Hkv, G)[:, :N]
    return out, lse


def window_step(q, cache_kv, k_new, v_new, window, dilation, bias, sink):
    S = q.shape[1]
    n_past = cache_kv.shape[1]
    idx = n_past + jnp.arange(S)[:, None] - dilation * jnp.arange(window + 1)[None, :]
    valid = idx >= 0
    past = cache_kv[:, jnp.clip(idx, 0, n_past - 1)]
    ni = jnp.clip(idx - n_past, 0, S - 1)
    sel = (idx >= n_past)[None, :, :, None, None]
    kg = jnp.where(sel, k_new[:, ni], past[:, :, :, 0])
    vg = jnp.where(sel, v_new[:, ni], past[:, :, :, 1])
    s = jnp.einsum('bshgd,bsjhd->bshgj', q, kg, preferred_element_type=jnp.float32) * (HEAD_DIM ** -0.5)
    if bias is not None:
        s = s + bias.astype(jnp.float32)
    p, lse = masked_softmax(s, valid[:, None, None, :], None if sink is None else sink[:, :, None])
    out = jnp.einsum('bshgj,bsjhd->bshgd', p, vg.astype(jnp.float32))
    return out, lse


def combine_groups(outs, lses):
    w = jax.nn.softmax(jnp.stack(lses), axis=0)
    return jnp.sum(w[..., None] * jnp.stack(outs), axis=0)


def dilated_prompt(q, k, v, g, bias):
    w, d = DILATIONS[g]
    Bn, S, H, dh = q.shape
    n = S // d

    def to_strided(t):
        return t.reshape(Bn, n, d, H, dh).transpose(0, 2, 1, 3, 4).reshape(Bn * d, n, H, dh)

    out, lse = band_attention(to_strided(q)[:, :, :, None], to_strided(k), to_strided(v), w // d, bias, None)
    out = out.reshape(Bn, d, n, H, dh).transpose(0, 2, 1, 3, 4).reshape(Bn, S, H, dh)
    lse = lse.reshape(Bn, d, n, H).transpose(0, 2, 1, 3).reshape(Bn, S, H)
    return out, lse


def mixer_prompt(qa, ka, va, qb, kb, vb, t5_table, sink):
    S = qa.shape[1]
    outs, lses, rows = [], [], []
    for g in range(N_DIL):
        o, s = dilated_prompt(qa[:, :, g], ka[:, :, g], va[:, :, g], g, dilated_bias(t5_table, g))
        outs.append(o)
        lses.append(s)
        keep = min(DILATIONS[g][0], S)
        rows.append(jnp.stack([ka[:, S - keep:, g], va[:, S - keep:, g]], axis=2))
    out_a = combine_groups(outs, lses)
    pos = jnp.arange(S)
    qb, kb = rope(qb, pos), rope(kb, pos)
    out_b, _ = band_attention(qb, kb, vb, B_WINDOW, None, sink)
    keep = min(B_WINDOW, S)
    rows.append(jnp.stack([kb[:, S - keep:], vb[:, S - keep:]], axis=2))
    return out_a, out_b, rows


def mixer_sample(qa, ka, va, qb, kb, vb, caches, t5_table, sink):
    S = qa.shape[1]
    outs, lses, rows = [], [], []
    for g in range(N_DIL):
        w, d = DILATIONS[g]
        o, s = window_step(qa[:, :, g][:, :, :, None], caches[g], ka[:, :, g], va[:, :, g],
                           w // d, d, dilated_bias(t5_table, g), None)
        outs.append(o[:, :, :, 0])
        lses.append(s[:, :, :, 0])
        rows.append(jnp.stack([ka[:, :, g], va[:, :, g]], axis=2))
    out_a = combine_groups(outs, lses)
    pos = PAST_LEN + jnp.arange(S)
    qb, kb = rope(qb, pos), rope(kb, pos)
    out_b, _ = window_step(qb, caches[N_DIL], kb, vb, B_WINDOW, 1, None, sink)
    rows.append(jnp.stack([kb, vb], axis=2))
    return out_a, out_b, rows


def trunk_layer(x, c, mixer_fn, w_ada, b_ada, w_in, w_pa, w_pb, w_o, w_gu, w_down, ln_g, ln_b):
    mod = (jax.nn.silu(c) @ w_ada + b_ada)[:, None, :]
    sh1, sc1, g1, sh2, sc2, g2 = jnp.split(mod, 6, axis=-1)
    u = x * (1 + sc1) + sh1
    h = u @ w_in
    Bn, S, _ = h.shape
    a = h[..., :A_COLS].reshape(Bn, S, N_DIL, 3, A_HEADS, HEAD_DIM)
    o = A_COLS
    qb = h[..., o:o + B_Q].reshape(Bn, S, B_KV_HEADS, B_GROUP, HEAD_DIM); o += B_Q
    kb = h[..., o:o + B_KV].reshape(Bn, S, B_KV_HEADS, HEAD_DIM); o += B_KV
    vb = h[..., o:o + B_KV].reshape(Bn, S, B_KV_HEADS, HEAD_DIM); o += B_KV
    gates = jax.nn.sigmoid(h[..., o:].astype(jnp.float32)).astype(x.dtype)
    out_a, out_b, rows = mixer_fn(a[:, :, :, 0], a[:, :, :, 1], a[:, :, :, 2], qb, kb, vb)
    ya = out_a.reshape(Bn, S, A_HEADS * HEAD_DIM).astype(x.dtype) @ w_pa
    yb = out_b.reshape(Bn, S, B_Q).astype(x.dtype) @ w_pb
    y = (gates[..., :D_MODEL] * ya + gates[..., D_MODEL:] * yb) @ w_o
    x = layer_norm(ALPHA * x + g1 * y, ln_g[0], ln_b[0])
    u2 = x * (1 + sc2) + sh2
    gu = u2 @ w_gu
    f = (jax.nn.silu(gu[..., :D_FF]) * gu[..., D_FF:]) @ w_down
    x = layer_norm(ALPHA * x + g2 * f, ln_g[1], ln_b[1])
    return x, rows


def setup_inputs(seed: int = 0) -> dict:
    key = jax.random.key(seed)
    ks = jax.random.split(key, 20)
    nrm = lambda k, shp, s=1.0: jax.random.normal(k, shp, jnp.float32) * s
    la = [min(w, PAST_LEN) for w, _ in DILATIONS]
    lb = min(B_WINDOW, PAST_LEN)
    return {
        "x_prompt": nrm(ks[0], (BATCH, SEQ, D_MODEL)),
        "x_sample": nrm(ks[1], (DEC_BATCH, DEC_SEQ, D_MODEL)),
        "c_prompt": nrm(ks[2], (BATCH, D_MODEL)),
        "c_sample": nrm(ks[3], (DEC_BATCH, D_MODEL)),
        "cache_a0": nrm(ks[4], (DEPTH, DEC_BATCH, la[0], 2, A_HEADS, HEAD_DIM)),
        "cache_a1": nrm(ks[5], (DEPTH, DEC_BATCH, la[1], 2, A_HEADS, HEAD_DIM)),
        "cache_a2": nrm(ks[6], (DEPTH, DEC_BATCH, la[2], 2, A_HEADS, HEAD_DIM)),
        "cache_b": nrm(ks[7], (DEPTH, DEC_BATCH, lb, 2, B_KV_HEADS, HEAD_DIM)),
        "t5_table": nrm(ks[8], (N_BUCKETS, N_DIL * A_HEADS), 0.5),
        "w_ada": nrm(ks[9], (DEPTH, D_MODEL, 6 * D_MODEL), 0.5 * D_MODEL ** -0.5),
        "b_ada": nrm(ks[10], (DEPTH, 6 * D_MODEL), 0.01),
        "w_in": nrm(ks[11], (DEPTH, D_MODEL, IN_COLS), D_MODEL ** -0.5),
        "sinks": nrm(ks[12], (DEPTH, B_KV_HEADS, B_GROUP)),
        "w_pa": nrm(ks[13], (DEPTH, A_HEADS * HEAD_DIM, D_MODEL), (A_HEADS * HEAD_DIM) ** -0.5),
        "w_pb": nrm(ks[14], (DEPTH, B_Q, D_MODEL), B_Q ** -0.5),
        "w_o": nrm(ks[15], (DEPTH, D_MODEL, D_MODEL), BETA * D_MODEL ** -0.5),
        "w_gu": nrm(ks[16], (DEPTH, D_MODEL, 2 * D_FF), D_MODEL ** -0.5),
        "w_down": nrm(ks[17], (DEPTH, D_FF, D_MODEL), BETA * D_FF ** -0.5),
        "ln_g": 1.0 + nrm(ks[18], (DEPTH, 2, D_MODEL), 0.01),
        "ln_b": nrm(ks[19], (DEPTH, 2, D_MODEL), 0.01),
    }


def reference(x_prompt, x_sample, c_prompt, c_sample, cache_a0, cache_a1, cache_a2, cache_b,
              t5_table, w_ada, b_ada, w_in, sinks, w_pa, w_pb, w_o, w_gu, w_down, ln_g, ln_b):
    caches = (cache_a0, cache_a1, cache_a2, cache_b)
    xp, xs = x_prompt, x_sample
    rows_p = [[] for _ in range(N_DIL + 1)]
    rows_s = [[] for _ in range(N_DIL + 1)]
    for l in range(DEPTH):
        wl = (w_ada[l], b_ada[l], w_in[l], w_pa[l], w_pb[l], w_o[l], w_gu[l], w_down[l], ln_g[l], ln_b[l])
        fp = functools.partial(mixer_prompt, t5_table=t5_table, sink=sinks[l])
        fs = functools.partial(mixer_sample, caches=[cc[l] for cc in caches], t5_table=t5_table, sink=sinks[l])
        xp, rp = trunk_layer(xp, c_prompt, fp, *wl)
        xs, rs = trunk_layer(xs, c_sample, fs, *wl)
        for i in range(N_DIL + 1):
            rows_p[i].append(rp[i])
            rows_s[i].append(rs[i])
    a0_p, a1_p, a2_p, b_p = [jnp.stack(r) for r in rows_p]
    a0_s, a1_s, a2_s, b_s = [jnp.stack(r) for r in rows_s]
    return (xp, xs, a0_p, a0_s, a1_p, a1_s, a2_p, a2_s, b_p, b_s)
```

```python
import functools
import math

import numpy as np
import jax
import jax.numpy as jnp
from jax import lax
from jax.experimental import pallas as pl
from jax.experimental.pallas import tpu as pltpu

F32 = jnp.float32
BF16 = jnp.bfloat16

D_MODEL = 2048
BATCH = 8
SEQ = 2048
DEPTH = 2
DEC_BATCH = 128
PAST_LEN = 8192
HEAD_DIM = 64
DILATIONS = ((128, 1), (512, 4), (2048, 16))
N_DIL = 3
A_HEADS = 8
B_HEADS = 16
B_KV_HEADS = 2
B_GROUP = B_HEADS // B_KV_HEADS
B_WINDOW = 128
ROPE_THETA = 150000.0
N_BUCKETS = 32
T5_MAX_DIST = 2048
D_FF = ((8 * D_MODEL + 3 * 256 - 1) // (3 * 256)) * 256
ALPHA = (2 * DEPTH) ** 0.25
LN_EPS = 1e-5
NEG_INF = -1e30

BLK = 128
A_HD = A_HEADS * HEAD_DIM
A_GROUP_COLS = 3 * A_HD
A_COLS = N_DIL * A_GROUP_COLS
B_Q = B_HEADS * HEAD_DIM
B_KV = B_KV_HEADS * HEAD_DIM
B_COLS = B_Q + 2 * B_KV
G_COLS = 2 * D_MODEL
LANES = 128
Q_SCALE = HEAD_DIM ** -0.5

_VMEM_LIMIT = 56 * 1024 * 1024


def _cparams(*sem):
    return pltpu.CompilerParams(dimension_semantics=sem, vmem_limit_bytes=_VMEM_LIMIT)


def _mod_spec(mod, tm, rows_per_mod):
    mr = mod.shape[2]
    return pl.BlockSpec((6, None, mr, D_MODEL), lambda i, *_: (0, (i * tm) // rows_per_mod, 0, 0))


def _layer_norm(z, g, b):
    mu = jnp.mean(z, axis=-1, keepdims=True)
    zc = z - mu
    var = jnp.mean(zc * zc, axis=-1, keepdims=True)
    return zc * lax.rsqrt(var + LN_EPS) * g + b


def _adaln_kernel(c_ref, w_ref, b_ref, o_ref):
    c = c_ref[...]
    s = (c * jax.nn.sigmoid(c)).astype(BF16)
    o_ref[...] = jnp.dot(s, w_ref[...].astype(BF16), preferred_element_type=F32) + b_ref[...]


def _adaln(c_all, w_ada, b_ada):
    rows = c_all.shape[0]
    tn = 1024
    return pl.pallas_call(
        _adaln_kernel,
        grid=(DEPTH, 6 * D_MODEL // tn),
        in_specs=[
            pl.BlockSpec((rows, D_MODEL), lambda l, j: (0, 0)),
            pl.BlockSpec((None, D_MODEL, tn), lambda l, j: (l, 0, j)),
            pl.BlockSpec((None, 1, tn), lambda l, j: (l, 0, j)),
        ],
        out_specs=pl.BlockSpec((None, rows, tn), lambda l, j: (l, 0, j)),
        out_shape=jax.ShapeDtypeStruct((DEPTH, rows, 6 * D_MODEL), F32),
        compiler_params=_cparams("parallel", "parallel"),
        name="adaln",
    )(c_all, w_ada, b_ada.reshape(DEPTH, 1, 6 * D_MODEL))


def _modulate(x_ref, mod_ref, shift_idx, scale_idx):
    return (x_ref[...] * (1.0 + mod_ref[scale_idx]) + mod_ref[shift_idx]).astype(BF16)


def _proj_a_kernel(x_ref, mod_ref, w_ref, q_ref, kv_ref, u_ref):
    @pl.when(pl.program_id(1) == 0)
    def _():
        u_ref[...] = _modulate(x_ref, mod_ref, 0, 1)

    acc = jnp.dot(u_ref[...], w_ref[...], preferred_element_type=F32)
    q_ref[...] = acc[:, :A_HD].astype(BF16)
    kv_ref[...] = acc[:, A_HD:]


def _proj_a(x, mod, w_a, tm, rows_per_mod):
    t = x.shape[0]
    return pl.pallas_call(
        _proj_a_kernel,
        grid=(t // tm, N_DIL),
        in_specs=[
            pl.BlockSpec((tm, D_MODEL), lambda i, g: (i, 0)),
            _mod_spec(mod, tm, rows_per_mod),
            pl.BlockSpec((D_MODEL, A_GROUP_COLS), lambda i, g: (0, g)),
        ],
        out_specs=[
            pl.BlockSpec((tm, A_HD), lambda i, g: (i, g)),
            pl.BlockSpec((tm, 2 * A_HD), lambda i, g: (i, g)),
        ],
        out_shape=[
            jax.ShapeDtypeStruct((t, N_DIL * A_HD), BF16),
            jax.ShapeDtypeStruct((t, N_DIL * 2 * A_HD), F32),
        ],
        scratch_shapes=[pltpu.VMEM((tm, D_MODEL), BF16)],
        compiler_params=_cparams("parallel", "arbitrary"),
        name="proj_a",
    )(x, mod, w_a)


def _proj_b_kernel(x_ref, mod_ref, w_ref, cos_ref, sin_ref, q_ref, kv_ref):
    u = _modulate(x_ref, mod_ref, 0, 1)
    acc = jnp.dot(u, w_ref[...], preferred_element_type=F32)
    cos = cos_ref[...]
    sin = sin_ref[...]
    lane = lax.broadcasted_iota(jnp.int32, cos.shape, 1)
    first_half = (lane % HEAD_DIM) < (HEAD_DIM // 2)

    def rope(v):
        partner = jnp.where(first_half,
                            pltpu.roll(v, LANES - HEAD_DIM // 2, axis=1),
                            pltpu.roll(v, HEAD_DIM // 2, axis=1))
        return v * cos + partner * sin

    for c in range(B_Q // LANES):
        q_ref[:, c * LANES:(c + 1) * LANES] = rope(acc[:, c * LANES:(c + 1) * LANES]).astype(BF16)
    kv_ref[:, :B_KV] = rope(acc[:, B_Q:B_Q + B_KV])
    kv_ref[:, B_KV:] = acc[:, B_Q + B_KV:]


def _proj_b(x, mod, w_b, cos_t, sin_t, tm, rows_per_mod):
    t = x.shape[0]
    n_pos_blocks = cos_t.shape[0] // tm
    return pl.pallas_call(
        _proj_b_kernel,
        grid=(t // tm,),
        in_specs=[
            pl.BlockSpec((tm, D_MODEL), lambda i: (i, 0)),
            _mod_spec(mod, tm, rows_per_mod),
            pl.BlockSpec((D_MODEL, B_COLS), lambda i: (0, 0)),
            pl.BlockSpec((tm, LANES), lambda i: (i % n_pos_blocks, 0)),
            pl.BlockSpec((tm, LANES), lambda i: (i % n_pos_blocks, 0)),
        ],
        out_specs=[
            pl.BlockSpec((tm, B_Q), lambda i: (i, 0)),
            pl.BlockSpec((tm, 2 * B_KV), lambda i: (i, 0)),
        ],
        out_shape=[
            jax.ShapeDtypeStruct((t, B_Q), BF16),
            jax.ShapeDtypeStruct((t, 2 * B_KV), F32),
        ],
        compiler_params=_cparams("parallel"),
        name="proj_b",
    )(x, mod, w_b, cos_t, sin_t)


def _gates_kernel(x_ref, mod_ref, w_ref, o_ref, u_ref):
    @pl.when(pl.program_id(1) == 0)
    def _():
        u_ref[...] = _modulate(x_ref, mod_ref, 0, 1)

    acc = jnp.dot(u_ref[...], w_ref[...], preferred_element_type=F32)
    o_ref[...] = jax.nn.sigmoid(acc)


def _gates(x, mod, w_g, tm, rows_per_mod):
    t = x.shape[0]
    tn = 1024
    return pl.pallas_call(
        _gates_kernel,
        grid=(t // tm, G_COLS // tn),
        in_specs=[
            pl.BlockSpec((tm, D_MODEL), lambda i, j: (i, 0)),
            _mod_spec(mod, tm, rows_per_mod),
            pl.BlockSpec((D_MODEL, tn), lambda i, j: (0, j)),
        ],
        out_specs=pl.BlockSpec((tm, tn), lambda i, j: (i, j)),
        out_shape=jax.ShapeDtypeStruct((t, G_COLS), F32),
        scratch_shapes=[pltpu.VMEM((tm, D_MODEL), BF16)],
        compiler_params=_cparams("parallel", "arbitrary"),
        name="gates",
    )(x, mod, w_g)


def _dot_nt(a, b):
    return lax.dot_general(a, b, (((1,), (1,)), ((), ())), preferred_element_type=F32)


def _band_masks():
    row = lax.broadcasted_iota(jnp.int32, (BLK, BLK), 0)
    col = lax.broadcasted_iota(jnp.int32, (BLK, BLK), 1)
    return col >= row, col <= row


def _band_a_kernel(q_ref, kp_ref, kc_ref, vp_ref, vc_ref, tp_ref, tc_ref, o_ref, l_ref):
    has_prev = pl.program_id(2) > 0
    valid_p, valid_c = _band_masks()
    valid_p = jnp.logical_and(valid_p, has_prev)
    lo = lax.broadcasted_iota(jnp.int32, (BLK, LANES), 1) < HEAD_DIM
    for hp in range(A_HEADS // 2):
        sl = slice(hp * LANES, (hp + 1) * LANES)
        qp = q_ref[:, sl]
        kp = kp_ref[:, sl].astype(BF16)
        kc = kc_ref[:, sl].astype(BF16)
        vp = vp_ref[:, sl].astype(BF16)
        vc = vc_ref[:, sl].astype(BF16)
        outs, lses = [], []
        for e in range(2):
            h = 2 * hp + e
            qm = jnp.where(lo if e == 0 else jnp.logical_not(lo), qp, jnp.zeros_like(qp))
            sp = jnp.where(valid_p, _dot_nt(qm, kp) + tp_ref[h], NEG_INF)
            sc = jnp.where(valid_c, _dot_nt(qm, kc) + tc_ref[h], NEG_INF)
            m = jnp.maximum(jnp.max(sp, axis=-1, keepdims=True), jnp.max(sc, axis=-1, keepdims=True))
            pp = jnp.exp(sp - m)
            pc = jnp.exp(sc - m)
            den = jnp.sum(pp, axis=-1, keepdims=True) + jnp.sum(pc, axis=-1, keepdims=True)
            pv = (jnp.dot(pp.astype(BF16), vp, preferred_element_type=F32)
                  + jnp.dot(pc.astype(BF16), vc, preferred_element_type=F32))
            outs.append(pv / den)
            lses.append(jnp.broadcast_to(m + jnp.log(den), (BLK, LANES)))
        o_ref[:, sl] = jnp.where(lo, outs[0], outs[1])
        l_ref[:, sl] = jnp.where(lo, lses[0], lses[1])


def _band_a(qa, kva, tp, tc, g):
    d = DILATIONS[g][1]
    n = SEQ // d
    nb = n // BLK
    qv = qa.reshape(BATCH, n, d * N_DIL * A_HD)
    kvv = kva.reshape(BATCH, n, d * N_DIL * 2 * A_HD)
    qcb = N_DIL
    kcb = N_DIL * 2
    blk = (None, BLK, A_HD)
    prev = lambda j: jnp.maximum(j - 1, 0)
    o, l = pl.pallas_call(
        _band_a_kernel,
        grid=(BATCH, d, nb),
        in_specs=[
            pl.BlockSpec(blk, lambda b, r, j: (b, j, r * qcb + g)),
            pl.BlockSpec(blk, lambda b, r, j: (b, prev(j), r * kcb + 2 * g)),
            pl.BlockSpec(blk, lambda b, r, j: (b, j, r * kcb + 2 * g)),
            pl.BlockSpec(blk, lambda b, r, j: (b, prev(j), r * kcb + 2 * g + 1)),
            pl.BlockSpec(blk, lambda b, r, j: (b, j, r * kcb + 2 * g + 1)),
            pl.BlockSpec((A_HEADS, BLK, BLK), lambda b, r, j: (0, 0, 0)),
            pl.BlockSpec((A_HEADS, BLK, BLK), lambda b, r, j: (0, 0, 0)),
        ],
        out_specs=[
            pl.BlockSpec(blk, lambda b, r, j: (b, j, r)),
            pl.BlockSpec(blk, lambda b, r, j: (b, j, r)),
        ],
        out_shape=[
            jax.ShapeDtypeStruct((BATCH, n, d * A_HD), F32),
            jax.ShapeDtypeStruct((BATCH, n, d * A_HD), F32),
        ],
        compiler_params=_cparams("parallel", "parallel", "arbitrary"),
        name=f"band_a{g}",
    )(qv, kvv, kvv, kvv, kvv, tp, tc)
    return o.reshape(BATCH * SEQ, A_HD), l.reshape(BATCH * SEQ, A_HD)


def _band_b_kernel(sink_ref, q_ref, kvp_ref, kvc_ref, o_ref):
    has_prev = pl.program_id(1) > 0
    valid_p, valid_c = _band_masks()
    valid_p = jnp.logical_and(valid_p, has_prev)
    lo = lax.broadcasted_iota(jnp.int32, (BLK, LANES), 1) < HEAD_DIM

    def variants(x):
        return x.astype(BF16), pltpu.roll(x, HEAD_DIM, axis=1).astype(BF16)

    kp = variants(kvp_ref[:, :B_KV])
    kc = variants(kvc_ref[:, :B_KV])
    vp = variants(kvp_ref[:, B_KV:])
    vc = variants(kvc_ref[:, B_KV:])
    for pair in range(B_HEADS // 2):
        sl = slice(pair * LANES, (pair + 1) * LANES)
        qp = q_ref[:, sl]
        outs = []
        for e in range(2):
            hq = 2 * pair + e
            hk = hq // B_GROUP
            sel = 0 if e == hk else 1
            qm = jnp.where(lo if e == 0 else jnp.logical_not(lo), qp, jnp.zeros_like(qp))
            sp = jnp.where(valid_p, _dot_nt(qm, kp[sel]), NEG_INF)
            sc = jnp.where(valid_c, _dot_nt(qm, kc[sel]), NEG_INF)
            sink = sink_ref[hq]
            m = jnp.maximum(jnp.max(sp, axis=-1, keepdims=True), jnp.max(sc, axis=-1, keepdims=True))
            m = jnp.maximum(m, sink)
            pp = jnp.exp(sp - m)
            pc = jnp.exp(sc - m)
            den = (jnp.sum(pp, axis=-1, keepdims=True) + jnp.sum(pc, axis=-1, keepdims=True)
                   + jnp.exp(sink - m))
            pv = (jnp.dot(pp.astype(BF16), vp[sel], preferred_element_type=F32)
                  + jnp.dot(pc.astype(BF16), vc[sel], preferred_element_type=F32))
            outs.append(pv / den)
        o_ref[:, sl] = jnp.where(lo, outs[0], outs[1]).astype(BF16)


def _band_b(qb, kvb, sink):
    nb = SEQ // BLK
    qv = qb.reshape(BATCH, SEQ, B_Q)
    kvv = kvb.reshape(BATCH, SEQ, 2 * B_KV)
    prev = lambda j: jnp.maximum(j - 1, 0)
    o = pl.pallas_call(
        _band_b_kernel,
        grid=(BATCH, nb),
        in_specs=[
            pl.BlockSpec(memory_space=pltpu.SMEM),
            pl.BlockSpec((None, BLK, B_Q), lambda b, j: (b, j, 0)),
            pl.BlockSpec((None, BLK, 2 * B_KV), lambda b, j: (b, prev(j), 0)),
            pl.BlockSpec((None, BLK, 2 * B_KV), lambda b, j: (b, j, 0)),
        ],
        out_specs=pl.BlockSpec((None, BLK, B_Q), lambda b, j: (b, j, 0)),
        out_shape=jax.ShapeDtypeStruct((BATCH, SEQ, B_Q), BF16),
        compiler_params=_cparams("parallel", "arbitrary"),
        name="band_b",
    )(sink, qv, kvv, kvv)
    return o.reshape(BATCH * SEQ, B_Q)


def _bf16_round(x):
    return x.astype(BF16).astype(F32)


def _sample_attn_kernel(qa_ref, kva_ref, qb_ref, kvb_ref, c0_ref, c1_ref, c2_ref, cb_ref,
                        bias_ref, bias_new_ref, sink_ref, oa_ref, ob_ref):
    bb = qa_ref.shape[0]
    head_a = (lax.broadcasted_iota(jnp.int32, (A_HEADS, A_HD), 1) // HEAD_DIM
              == lax.broadcasted_iota(jnp.int32, (A_HEADS, A_HD), 0))
    head_b = (lax.broadcasted_iota(jnp.int32, (B_HEADS, B_KV), 1) // HEAD_DIM
              == lax.broadcasted_iota(jnp.int32, (B_HEADS, B_KV), 0) // B_GROUP)
    caches = (c0_ref, c1_ref, c2_ref)

    def body(i, carry):
        outs, lses = [], []
        for g in range(N_DIL):
            q = qa_ref[i, :, g * A_HD:(g + 1) * A_HD].astype(F32)
            qbd = jnp.where(head_a, jnp.broadcast_to(q, (A_HEADS, A_HD)), 0.0)
            kc = caches[g][i, :, :A_HD].astype(BF16)
            vc = caches[g][i, :, A_HD:].astype(BF16)
            k_new = _bf16_round(kva_ref[i, :, 2 * g * A_HD:(2 * g + 1) * A_HD])
            v_new = _bf16_round(kva_ref[i, :, (2 * g + 1) * A_HD:(2 * g + 2) * A_HD])
            s = _dot_nt(qbd.astype(BF16), kc) + bias_ref[g]
            s_new = jnp.sum(qbd * k_new, axis=-1, keepdims=True) + bias_new_ref[g]
            m = jnp.maximum(jnp.max(s, axis=-1, keepdims=True), s_new)
            p = jnp.exp(s - m)
            p_new = jnp.exp(s_new - m)
            den = jnp.sum(p, axis=-1, keepdims=True) + p_new
            pv = jnp.dot(p.astype(BF16), vc, preferred_element_type=F32) + _bf16_round(p_new) * v_new
            outs.append(pv / den)
            lses.append(m + jnp.log(den))
        mx = jnp.maximum(jnp.maximum(lses[0], lses[1]), lses[2])
        es = [jnp.exp(l - mx) for l in lses]
        comb = (es[0] * outs[0] + es[1] * outs[1] + es[2] * outs[2]) / (es[0] + es[1] + es[2])
        oa_ref[i] = jnp.sum(jnp.where(head_a, comb, 0.0), axis=0, keepdims=True)

        qb = qb_ref[i]
        kc = cb_ref[i, :, :B_KV].astype(BF16)
        vc = cb_ref[i, :, B_KV:].astype(BF16)
        k_new = _bf16_round(kvb_ref[i, :, :B_KV])
        v_new = _bf16_round(kvb_ref[i, :, B_KV:])
        s = _dot_nt(qb, kc)
        s_new = jnp.sum(qb.astype(F32) * k_new, axis=-1, keepdims=True)
        sink = sink_ref[...]
        m = jnp.maximum(jnp.maximum(jnp.max(s, axis=-1, keepdims=True), s_new), sink)
        p = jnp.exp(s - m)
        p_new = jnp.exp(s_new - m)
        den = jnp.sum(p, axis=-1, keepdims=True) + p_new + jnp.exp(sink - m)
        pv = jnp.dot(p.astype(BF16), vc, preferred_element_type=F32) + _bf16_round(p_new) * v_new
        ob_ref[i] = jnp.where(head_b, pv / den, 0.0)
        return carry

    lax.fori_loop(0, bb, body, 0)


def _sample_attn(qa, kva, qb, kvb, caches_l, bias_s, bias_new, sink):
    n = DEC_BATCH
    bb = 8
    qa3 = qa.reshape(n, 1, N_DIL * A_HD)
    kva3 = kva.reshape(n, 1, N_DIL * 2 * A_HD)
    kvb3 = kvb.reshape(n, 1, 2 * B_KV)
    q4 = qb.reshape(n, B_KV_HEADS, B_GROUP, HEAD_DIM)
    z = jnp.zeros_like(q4[:, 0])
    qb2 = jnp.concatenate([jnp.concatenate([q4[:, 0], z], axis=-1),
                           jnp.concatenate([z, q4[:, 1]], axis=-1)], axis=1)
    cviews = [c.reshape(n, BLK, -1) for c in caches_l[:N_DIL]]
    cb = caches_l[N_DIL].reshape(n, BLK, 2 * B_KV)
    row3 = lambda w: pl.BlockSpec((bb, 1, w), lambda i: (i, 0, 0))
    oa, ob = pl.pallas_call(
        _sample_attn_kernel,
        grid=(n // bb,),
        in_specs=[
            row3(N_DIL * A_HD),
            row3(N_DIL * 2 * A_HD),
            pl.BlockSpec((bb, B_HEADS, B_KV), lambda i: (i, 0, 0)),
            row3(2 * B_KV),
            pl.BlockSpec((bb, BLK, 2 * A_HD), lambda i: (i, 0, 0)),
            pl.BlockSpec((bb, BLK, 2 * A_HD), lambda i: (i, 0, 0)),
            pl.BlockSpec((bb, BLK, 2 * A_HD), lambda i: (i, 0, 0)),
            pl.BlockSpec((bb, BLK, 2 * B_KV), lambda i: (i, 0, 0)),
            pl.BlockSpec((N_DIL, A_HEADS, BLK), lambda i: (0, 0, 0)),
            pl.BlockSpec((N_DIL, A_HEADS, 1), lambda i: (0, 0, 0)),
            pl.BlockSpec((B_HEADS, 1), lambda i: (0, 0)),
        ],
        out_specs=[
            pl.BlockSpec((bb, 1, A_HD), lambda i: (i, 0, 0)),
            pl.BlockSpec((bb, B_HEADS, B_KV), lambda i: (i, 0, 0)),
        ],
        out_shape=[
            jax.ShapeDtypeStruct((n, 1, A_HD), F32),
            jax.ShapeDtypeStruct((n, B_HEADS, B_KV), F32),
        ],
        compiler_params=_cparams("parallel"),
        name="sample_attn",
    )(qa3, kva3, qb2, kvb3, cviews[0], cviews[1], cviews[2], cb, bias_s, bias_new, sink.reshape(B_HEADS, 1))
    out_a = oa.reshape(n, A_HD)
    out_b = jnp.stack([ob[:, :B_GROUP, :HEAD_DIM], ob[:, B_GROUP:, HEAD_DIM:]], axis=1).reshape(n, B_Q)
    return out_a, out_b.astype(BF16)


def _post_attn_core(a, ob_ref, gate_ref, x_ref, mod_ref, wpa_ref, wpb_ref, wo_ref, lng_ref, lnb_ref, o_ref):
    ya = jnp.dot(a.astype(BF16), wpa_ref[...], preferred_element_type=F32)
    yb = jnp.dot(ob_ref[...], wpb_ref[...], preferred_element_type=F32)
    mix = gate_ref[:, :D_MODEL] * ya + gate_ref[:, D_MODEL:] * yb
    y = jnp.dot(mix.astype(BF16), wo_ref[...], preferred_element_type=F32)
    z = ALPHA * x_ref[...] + mod_ref[2] * y
    o_ref[...] = _layer_norm(z, lng_ref[0:1, :], lnb_ref[0:1, :])


def _post_attn_merge_kernel(o0_ref, l0_ref, o1_ref, l1_ref, o2_ref, l2_ref, *rest):
    l0, l1, l2 = l0_ref[...], l1_ref[...], l2_ref[...]
    mx = jnp.maximum(jnp.maximum(l0, l1), l2)
    e0, e1, e2 = jnp.exp(l0 - mx), jnp.exp(l1 - mx), jnp.exp(l2 - mx)
    a = (e0 * o0_ref[...] + e1 * o1_ref[...] + e2 * o2_ref[...]) / (e0 + e1 + e2)
    _post_attn_core(a, *rest)


def _post_attn_direct_kernel(a_ref, *rest):
    _post_attn_core(a_ref[...], *rest)


def _post_attn(a_parts, ob, gates, x, mod, w_pa, w_pb, w_o, ln_g, ln_b, tm, rows_per_mod):
    t = x.shape[0]
    row = lambda w: pl.BlockSpec((tm, w), lambda i: (i, 0))
    const = lambda shape: pl.BlockSpec(shape, lambda i: (0, 0), pipeline_mode=pl.Buffered(1))
    body = _post_attn_merge_kernel if len(a_parts) > 1 else _post_attn_direct_kernel
    return pl.pallas_call(
        body,
        grid=(t // tm,),
        in_specs=[row(A_HD)] * len(a_parts) + [
            row(B_Q), row(G_COLS), row(D_MODEL),
            _mod_spec(mod, tm, rows_per_mod),
            const((A_HD, D_MODEL)), const((B_Q, D_MODEL)), const((D_MODEL, D_MODEL)),
            const((2, D_MODEL)), const((2, D_MODEL)),
        ],
        out_specs=row(D_MODEL),
        out_shape=jax.ShapeDtypeStruct((t, D_MODEL), F32),
        compiler_params=_cparams("parallel"),
        name="post_attn",
    )(*a_parts, ob, gates, x, mod, w_pa, w_pb, w_o, ln_g, ln_b)


def _ffn_kernel(x_ref, mod_ref, wg_ref, wu_ref, wd_ref, lng_ref, lnb_ref, o_ref, u_ref, acc_ref):
    k = pl.program_id(1)

    @pl.when(k == 0)
    def _():
        u_ref[...] = _modulate(x_ref, mod_ref, 3, 4)
        acc_ref[...] = jnp.zeros_like(acc_ref)

    u = u_ref[...]
    gate = jnp.dot(u, wg_ref[...], preferred_element_type=F32)
    up = jnp.dot(u, wu_ref[...], preferred_element_type=F32)
    act = (gate * jax.nn.sigmoid(gate) * up).astype(BF16)
    acc_ref[...] += jnp.dot(act, wd_ref[...], preferred_element_type=F32)

    @pl.when(k == pl.num_programs(1) - 1)
    def _():
        z = ALPHA * x_ref[...] + mod_ref[5] * acc_ref[...]
        o_ref[...] = _layer_norm(z, lng_ref[1:2, :], lnb_ref[1:2, :])


def _ffn(x, mod, w_gu, w_down, ln_g, ln_b, tm, rows_per_mod):
    t = x.shape[0]
    tf = 512
    nf = D_FF // tf
    return pl.pallas_call(
        _ffn_kernel,
        grid=(t // tm, nf),
        in_specs=[
            pl.BlockSpec((tm, D_MODEL), lambda i, k: (i, 0)),
            _mod_spec(mod, tm, rows_per_mod),
            pl.BlockSpec((D_MODEL, tf), lambda i, k: (0, k)),
            pl.BlockSpec((D_MODEL, tf), lambda i, k: (0, k + nf)),
            pl.BlockSpec((tf, D_MODEL), lambda i, k: (k, 0)),
            pl.BlockSpec((2, D_MODEL), lambda i, k: (0, 0)),
            pl.BlockSpec((2, D_MODEL), lambda i, k: (0, 0)),
        ],
        out_specs=pl.BlockSpec((tm, D_MODEL), lambda i, k: (i, 0)),
        out_shape=jax.ShapeDtypeStruct((t, D_MODEL), F32),
        scratch_shapes=[pltpu.VMEM((tm, D_MODEL), BF16), pltpu.VMEM((tm, D_MODEL), F32)],
        compiler_params=_cparams("parallel", "arbitrary"),
        name="ffn",
    )(x, mod, w_gu, w_gu, w_down, ln_g, ln_b)


def _t5_bucket(dist):
    exact = N_BUCKETS // 2
    n = jnp.maximum(dist, 0)
    log_ratio = jnp.log(jnp.maximum(n, exact).astype(F32) / exact) / math.log(T5_MAX_DIST / exact)
    large = jnp.minimum(exact + (log_ratio * (N_BUCKETS - exact)).astype(jnp.int32), N_BUCKETS - 1)
    return jnp.where(n < exact, n, large)


def _bias_tables(t5_table):
    row = np.arange(BLK)[:, None]
    col = np.arange(BLK)[None, :]
    idx_prev = np.clip(BLK + row - col, 0, BLK)
    idx_cur = np.clip(row - col, 0, BLK)
    tps, tcs, bias_s, bias_new = [], [], [], []
    for g, (w, d) in enumerate(DILATIONS):
        bias = t5_table[_t5_bucket(d * jnp.arange(w // d + 1)), g * A_HEADS:(g + 1) * A_HEADS].T
        tps.append(bias[:, idx_prev])
        tcs.append(bias[:, idx_cur])
        bias_s.append(bias[:, BLK:0:-1])
        bias_new.append(bias[:, 0:1])
    return tps, tcs, jnp.stack(bias_s), jnp.stack(bias_new)


def _rope_tables(pos):
    half = HEAD_DIM // 2
    inv = ROPE_THETA ** (-jnp.arange(half, dtype=F32) / half)
    ang = pos.astype(F32)[:, None] * inv[None]
    cos, sin = jnp.cos(ang), jnp.sin(ang)
    cos_t = jnp.concatenate([cos, cos, cos, cos], axis=-1)
    sin_t = jnp.concatenate([-sin, sin, -sin, sin], axis=-1)
    return cos_t, sin_t


def kernel(x_prompt, x_sample, c_prompt, c_sample, cache_a0, cache_a1, cache_a2, cache_b, t5_table, w_ada, b_ada,
           w_in, sinks, w_pa, w_pb, w_o, w_gu, w_down, ln_g, ln_b):
    tp_rows = BATCH * SEQ
    mod = _adaln(jnp.concatenate([c_prompt, c_sample], axis=0), w_ada, b_ada)
    mod_p = mod[:, :BATCH].reshape(DEPTH, BATCH, 6, 1, D_MODEL).transpose(0, 2, 1, 3, 4)
    mod_s = mod[:, BATCH:].reshape(DEPTH, 1, DEC_BATCH, 6, D_MODEL).transpose(0, 3, 1, 2, 4)

    q_scale = jnp.concatenate([jnp.full((A_HD,), Q_SCALE, F32), jnp.ones((2 * A_HD,), F32)])
    w_a = (w_in[:, :, :A_COLS] * jnp.tile(q_scale, N_DIL)).astype(BF16)
    b_scale = jnp.concatenate([jnp.full((B_Q,), Q_SCALE, F32), jnp.ones((2 * B_KV,), F32)])
    w_b = (w_in[:, :, A_COLS:A_COLS + B_COLS] * b_scale).astype(BF16)
    w_g = w_in[:, :, A_COLS + B_COLS:].astype(BF16)
    w_pa_h, w_pb_h, w_o_h = w_pa.astype(BF16), w_pb.astype(BF16), w_o.astype(BF16)
    w_gu_h, w_down_h = w_gu.astype(BF16), w_down.astype(BF16)

    tps, tcs, bias_s, bias_new = _bias_tables(t5_table)
    cos_p, sin_p = _rope_tables(jnp.arange(SEQ))
    cos_s, sin_s = _rope_tables(jnp.full((DEC_BATCH,), PAST_LEN))
    caches = (cache_a0, cache_a1, cache_a2, cache_b)

    xp = x_prompt.reshape(tp_rows, D_MODEL)
    xs = x_sample.reshape(DEC_BATCH, D_MODEL)
    kva_p, kva_s, kvb_p, kvb_s = [], [], [], []
    for l in range(DEPTH):
        sink = sinks[l].reshape(B_HEADS)

        qa, kva = _proj_a(xp, mod_p[l], w_a[l], 512, SEQ)
        qb, kvb = _proj_b(xp, mod_p[l], w_b[l], cos_p, sin_p, 512, SEQ)
        gates = _gates(xp, mod_p[l], w_g[l], 512, SEQ)
        parts = []
        for g in range(N_DIL):
            parts.extend(_band_a(qa, kva, tps[g], tcs[g], g))
        ob = _band_b(qb, kvb, sink)
        xp = _post_attn(parts, ob, gates, xp, mod_p[l], w_pa_h[l], w_pb_h[l], w_o_h[l], ln_g[l], ln_b[l], 256, SEQ)
        xp = _ffn(xp, mod_p[l], w_gu_h[l], w_down_h[l], ln_g[l], ln_b[l], 512, SEQ)
        kva_p.append(kva)
        kvb_p.append(kvb)

        qa, kva = _proj_a(xs, mod_s[l], w_a[l], DEC_BATCH, DEC_BATCH)
        qb, kvb = _proj_b(xs, mod_s[l], w_b[l], cos_s, sin_s, DEC_BATCH, DEC_BATCH)
        gates = _gates(xs, mod_s[l], w_g[l], DEC_BATCH, DEC_BATCH)
        oa, ob = _sample_attn(qa, kva, qb, kvb, [c[l] for c in caches], bias_s, bias_new, sink)
        xs = _post_attn([oa], ob, gates, xs, mod_s[l], w_pa_h[l], w_pb_h[l], w_o_h[l], ln_g[l], ln_b[l],
                        DEC_BATCH, DEC_BATCH)
        xs = _ffn(xs, mod_s[l], w_gu_h[l], w_down_h[l], ln_g[l], ln_b[l], DEC_BATCH, DEC_BATCH)
        kva_s.append(kva)
        kvb_s.append(kvb)

    kva_p = jnp.stack(kva_p).reshape(DEPTH, BATCH, SEQ, N_DIL, 2, A_HEADS, HEAD_DIM)
    kva_s = jnp.stack(kva_s).reshape(DEPTH, DEC_BATCH, 1, N_DIL, 2, A_HEADS, HEAD_DIM)
    kvb_p = jnp.stack(kvb_p).reshape(DEPTH, BATCH, SEQ, 2, B_KV_HEADS, HEAD_DIM)
    kvb_s = jnp.stack(kvb_s).reshape(DEPTH, DEC_BATCH, 1, 2, B_KV_HEADS, HEAD_DIM)
    outs = [xp.reshape(BATCH, SEQ, D_MODEL), xs.reshape(DEC_BATCH, 1, D_MODEL)]
    for g, (w, _) in enumerate(DILATIONS):
        keep = min(w, SEQ)
        outs.append(kva_p[:, :, SEQ - keep:, g])
        outs.append(kva_s[:, :, :, g])
    outs.append(kvb_p[:, :, SEQ - min(B_WINDOW, SEQ):])
    outs.append(kvb_s)
    return tuple(outs)
```

```python
import math

import jax
import jax.numpy as jnp
from jax import lax
from jax.experimental import pallas as pl
from jax.experimental.pallas import tpu as pltpu

F32 = jnp.float32
BF16 = jnp.bfloat16

D_MODEL = 2048
BATCH = 8
SEQ = 2048
DEPTH = 2
DEC_BATCH = 128
PAST_LEN = 8192
HEAD_DIM = 64
DILATIONS = ((128, 1), (512, 4), (2048, 16))
N_DIL = 3
A_HEADS = 8
B_HEADS = 16
B_KV_HEADS = 2
B_GROUP = B_HEADS // B_KV_HEADS
B_WINDOW = 128
ROPE_THETA = 150000.0
N_BUCKETS = 32
T5_MAX_DIST = 2048
D_FF = ((8 * D_MODEL + 3 * 256 - 1) // (3 * 256)) * 256
ALPHA = (2 * DEPTH) ** 0.25
LN_EPS = 1e-5
NEG_INF = -1e30

BLK = 128
A_HD = A_HEADS * HEAD_DIM
A_GROUP_COLS = 3 * A_HD
A_COLS = N_DIL * A_GROUP_COLS
B_Q = B_HEADS * HEAD_DIM
B_KV = B_KV_HEADS * HEAD_DIM
B_COLS = B_Q + 2 * B_KV
G_COLS = 2 * D_MODEL
LANES = 128
PAIRS_PER_STEP = 2
Q_SCALE = HEAD_DIM ** -0.5
A_KEEP = tuple(min(w, SEQ) for w, _ in DILATIONS)
A_PAST = tuple(min(w, PAST_LEN) for w, _ in DILATIONS)

_VMEM_LIMIT = 56 * 1024 * 1024


def _cparams(*sem):
    return pltpu.CompilerParams(dimension_semantics=sem, vmem_limit_bytes=_VMEM_LIMIT)


def _mod_spec(mod, tm, rows_per_mod):
    mr = mod.shape[2]
    return pl.BlockSpec((6, None, mr, D_MODEL), lambda i, *_: (0, (i * tm) // rows_per_mod, 0, 0))


def _layer_norm(z, g, b):
    mu = jnp.mean(z, axis=-1, keepdims=True)
    zc = z - mu
    var = jnp.mean(zc * zc, axis=-1, keepdims=True)
    return zc * lax.rsqrt(var + LN_EPS) * g + b


def _adaln_kernel(c_ref, w_ref, b_ref, o_ref):
    c = c_ref[...]
    s = (c * jax.nn.sigmoid(c)).astype(BF16)
    o_ref[...] = jnp.dot(s, w_ref[...].astype(BF16), preferred_element_type=F32) + b_ref[...]


def _adaln(c_all, w_ada, b_ada):
    rows = c_all.shape[0]
    tn = 1024
    return pl.pallas_call(
        _adaln_kernel,
        grid=(DEPTH, 6 * D_MODEL // tn),
        in_specs=[
            pl.BlockSpec((rows, D_MODEL), lambda l, j: (0, 0)),
            pl.BlockSpec((None, D_MODEL, tn), lambda l, j: (l, 0, j)),
            pl.BlockSpec((None, 1, tn), lambda l, j: (l, 0, j)),
        ],
        out_specs=pl.BlockSpec((None, rows, tn), lambda l, j: (l, 0, j)),
        out_shape=jax.ShapeDtypeStruct((DEPTH, rows, 6 * D_MODEL), F32),
        compiler_params=_cparams("parallel", "parallel"),
        name="adaln",
    )(c_all, w_ada, b_ada.reshape(DEPTH, 1, 6 * D_MODEL))


def _modulate(x_ref, mod_ref, shift_idx, scale_idx):
    return (x_ref[...] * (1.0 + mod_ref[scale_idx]) + mod_ref[shift_idx]).astype(BF16)


def _proj_a_kernel(x_ref, mod_ref, w_ref, o_ref, u_ref):
    @pl.when(pl.program_id(1) == 0)
    def _():
        u_ref[...] = _modulate(x_ref, mod_ref, 0, 1)

    o_ref[...] = jnp.dot(u_ref[...], w_ref[...], preferred_element_type=F32)


def _proj_a(x, mod, w_a, tm, rows_per_mod):
    t = x.shape[0]
    return pl.pallas_call(
        _proj_a_kernel,
        grid=(t // tm, N_DIL),
        in_specs=[
            pl.BlockSpec((tm, D_MODEL), lambda i, g: (i, 0)),
            _mod_spec(mod, tm, rows_per_mod),
            pl.BlockSpec((D_MODEL, A_GROUP_COLS), lambda i, g: (0, g)),
        ],
        out_specs=pl.BlockSpec((tm, A_GROUP_COLS), lambda i, g: (i, g)),
        out_shape=jax.ShapeDtypeStruct((t, A_COLS), F32),
        scratch_shapes=[pltpu.VMEM((tm, D_MODEL), BF16)],
        compiler_params=_cparams("parallel", "arbitrary"),
        name="proj_a",
    )(x, mod, w_a)


def _proj_b_kernel(x_ref, mod_ref, w_ref, cos_ref, sin_ref, q_ref, kv_ref):
    u = _modulate(x_ref, mod_ref, 0, 1)
    acc = jnp.dot(u, w_ref[...], preferred_element_type=F32)
    cos = cos_ref[...]
    sin = sin_ref[...]
    lane = lax.broadcasted_iota(jnp.int32, cos.shape, 1)
    first_half = (lane % HEAD_DIM) < (HEAD_DIM // 2)

    def rope(v):
        partner = jnp.where(first_half,
                            pltpu.roll(v, LANES - HEAD_DIM // 2, axis=1),
                            pltpu.roll(v, HEAD_DIM // 2, axis=1))
        return v * cos + partner * sin

    for c in range(B_Q // LANES):
        q_ref[:, c * LANES:(c + 1) * LANES] = rope(acc[:, c * LANES:(c + 1) * LANES]).astype(BF16)
    kv_ref[:, :B_KV] = rope(acc[:, B_Q:B_Q + B_KV])
    kv_ref[:, B_KV:] = acc[:, B_Q + B_KV:]


def _proj_b(x, mod, w_b, cos_t, sin_t, tm, rows_per_mod):
    t = x.shape[0]
    n_pos_blocks = cos_t.shape[0] // tm
    return pl.pallas_call(
        _proj_b_kernel,
        grid=(t // tm,),
        in_specs=[
            pl.BlockSpec((tm, D_MODEL), lambda i: (i, 0)),
            _mod_spec(mod, tm, rows_per_mod),
            pl.BlockSpec((D_MODEL, B_COLS), lambda i: (0, 0)),
            pl.BlockSpec((tm, LANES), lambda i: (i % n_pos_blocks, 0)),
            pl.BlockSpec((tm, LANES), lambda i: (i % n_pos_blocks, 0)),
        ],
        out_specs=[
            pl.BlockSpec((tm, B_Q), lambda i: (i, 0)),
            pl.BlockSpec((tm, 2 * B_KV), lambda i: (i, 0)),
        ],
        out_shape=[
            jax.ShapeDtypeStruct((t, B_Q), BF16),
            jax.ShapeDtypeStruct((t, 2 * B_KV), F32),
        ],
        compiler_params=_cparams("parallel"),
        name="proj_b",
    )(x, mod, w_b, cos_t, sin_t)


def _gates_kernel(x_ref, mod_ref, w_ref, o_ref, u_ref):
    @pl.when(pl.program_id(1) == 0)
    def _():
        u_ref[...] = _modulate(x_ref, mod_ref, 0, 1)

    acc = jnp.dot(u_ref[...], w_ref[...], preferred_element_type=F32)
    o_ref[...] = jax.nn.sigmoid(acc)


def _gates(x, mod, w_g, tm, rows_per_mod):
    t = x.shape[0]
    tn = 1024
    return pl.pallas_call(
        _gates_kernel,
        grid=(t // tm, G_COLS // tn),
        in_specs=[
            pl.BlockSpec((tm, D_MODEL), lambda i, j: (i, 0)),
            _mod_spec(mod, tm, rows_per_mod),
            pl.BlockSpec((D_MODEL, tn), lambda i, j: (0, j)),
        ],
        out_specs=pl.BlockSpec((tm, tn), lambda i, j: (i, j)),
        out_shape=jax.ShapeDtypeStruct((t, G_COLS), F32),
        scratch_shapes=[pltpu.VMEM((tm, D_MODEL), BF16)],
        compiler_params=_cparams("parallel", "arbitrary"),
        name="gates",
    )(x, mod, w_g)


def _dot_nt(a, b):
    return lax.dot_general(a, b, (((1,), (1,)), ((), ())), preferred_element_type=F32)


def _band_masks(rows):
    row = lax.broadcasted_iota(jnp.int32, (rows, BLK), 0) % BLK
    col = lax.broadcasted_iota(jnp.int32, (rows, BLK), 1)
    return col >= row, col <= row


def _rows(start, d):
    return pl.ds(start, BLK, stride=d) if d > 1 else pl.ds(pl.multiple_of(start, BLK), BLK)


def _dil_group(gi, q_refs, k_refs, v_refs, tp_ref, tc_ref, acc_ref, m_ref, l_ref):
    d = DILATIONS[gi][1]
    nblk = SEQ // (BLK * d)
    valid_p0, valid_c = _band_masks(2 * BLK)
    lo = lax.broadcasted_iota(jnp.int32, (BLK, LANES), 1) < HEAD_DIM
    zero = jnp.zeros((BLK, LANES), BF16)

    def unit(u, carry):
        r = u // nblk
        j = u % nblk
        start = j * (BLK * d) + r
        start_prev = jnp.maximum(j - 1, 0) * (BLK * d) + r
        valid_p = jnp.logical_and(valid_p0, j > 0)
        for p in range(PAIRS_PER_STEP):
            q = q_refs[p][_rows(start, d), :].astype(BF16)
            qs = jnp.concatenate([jnp.where(lo, q, zero), jnp.where(lo, zero, q)], axis=0)
            kp = k_refs[p][_rows(start_prev, d), :].astype(BF16)
            kc = k_refs[p][_rows(start, d), :].astype(BF16)
            vp = v_refs[p][_rows(start_prev, d), :].astype(BF16)
            vc = v_refs[p][_rows(start, d), :].astype(BF16)
            tp = tp_ref[2 * p:2 * p + 2].reshape(2 * BLK, BLK)
            tc = tc_ref[2 * p:2 * p + 2].reshape(2 * BLK, BLK)
            sp = jnp.where(valid_p, _dot_nt(qs, kp) + tp, NEG_INF)
            sc = jnp.where(valid_c, _dot_nt(qs, kc) + tc, NEG_INF)
            m = jnp.maximum(jnp.max(sp, axis=-1, keepdims=True), jnp.max(sc, axis=-1, keepdims=True))
            pp = jnp.exp(sp - m)
            pc = jnp.exp(sc - m)
            den = jnp.sum(pp, axis=-1, keepdims=True) + jnp.sum(pc, axis=-1, keepdims=True)
            pv = (jnp.dot(pp.astype(BF16), vp, preferred_element_type=F32)
                  + jnp.dot(pc.astype(BF16), vc, preferred_element_type=F32))
            num = jnp.where(lo, pv[:BLK], pv[BLK:])
            m_g = jnp.where(lo, jnp.broadcast_to(m[:BLK], (BLK, LANES)), jnp.broadcast_to(m[BLK:], (BLK, LANES)))
            l_g = jnp.where(lo, jnp.broadcast_to(den[:BLK], (BLK, LANES)), jnp.broadcast_to(den[BLK:], (BLK, LANES)))
            rows = _rows(start, d)
            if gi == 0:
                acc_ref[p, rows, :] = num
                m_ref[p, rows, :] = m_g
                l_ref[p, rows, :] = l_g
            else:
                m_old = m_ref[p, rows, :]
                m_new = jnp.maximum(m_old, m_g)
                a = jnp.exp(m_old - m_new)
                b = jnp.exp(m_g - m_new)
                acc_ref[p, rows, :] = acc_ref[p, rows, :] * a + num * b
                l_ref[p, rows, :] = l_ref[p, rows, :] * a + l_g * b
                m_ref[p, rows, :] = m_new
        return carry

    lax.fori_loop(0, SEQ // BLK, unit, 0)


def _dil_attn_kernel(q0, q1, k0, k1, v0, v1, tp_ref, tc_ref, o_ref, c0_ref, c1_ref, c2_ref, acc_ref, m_ref, l_ref):
    g = pl.program_id(2)
    q_refs, k_refs, v_refs = (q0, q1), (k0, k1), (v0, v1)
    c_refs = (c0_ref, c1_ref, c2_ref)
    for gi in range(N_DIL):
        @pl.when(g == gi)
        def _(gi=gi):
            _dil_group(gi, q_refs, k_refs, v_refs, tp_ref, tc_ref, acc_ref, m_ref, l_ref)
            keep = A_KEEP[gi]
            for p in range(PAIRS_PER_STEP):
                c_refs[gi][0, 2 * p:2 * p + 2] = k_refs[p][SEQ - keep:, :].T.reshape(2, HEAD_DIM, keep)
                c_refs[gi][1, 2 * p:2 * p + 2] = v_refs[p][SEQ - keep:, :].T.reshape(2, HEAD_DIM, keep)

    @pl.when(g == N_DIL - 1)
    def _():
        for p in range(PAIRS_PER_STEP):
            o_ref[:, p * LANES:(p + 1) * LANES] = (acc_ref[p] / l_ref[p]).astype(BF16)


def _dil_attn(h_a, tps, tcs):
    hv = h_a.reshape(BATCH, SEQ, A_COLS)
    steps = A_HEADS // 2 // PAIRS_PER_STEP
    heads = 2 * PAIRS_PER_STEP
    slab = (None, SEQ, LANES)
    col = lambda part, p: (lambda b, s, g: (b, 0, g * 12 + part * 4 + s * PAIRS_PER_STEP + p))
    tbl = pl.BlockSpec((None, heads, BLK, BLK), lambda b, s, g: (g, s, 0, 0))
    cache = lambda keep: pl.BlockSpec((None, 2, heads, HEAD_DIM, keep), lambda b, s, g: (b, 0, s, 0, 0))
    outs = pl.pallas_call(
        _dil_attn_kernel,
        grid=(BATCH, steps, N_DIL),
        in_specs=[pl.BlockSpec(slab, col(part, p)) for part in range(3) for p in range(PAIRS_PER_STEP)] + [tbl, tbl],
        out_specs=[pl.BlockSpec((None, SEQ, heads * HEAD_DIM), lambda b, s, g: (b, 0, s))]
        + [cache(keep) for keep in A_KEEP],
        out_shape=[jax.ShapeDtypeStruct((BATCH, SEQ, A_HD), BF16)]
        + [jax.ShapeDtypeStruct((BATCH, 2, A_HEADS, HEAD_DIM, keep), F32) for keep in A_KEEP],
        scratch_shapes=[pltpu.VMEM((PAIRS_PER_STEP, SEQ, LANES), F32)] * 3,
        compiler_params=_cparams("parallel", "parallel", "arbitrary"),
        name="dil_attn",
    )(*([hv] * (3 * PAIRS_PER_STEP)), tps, tcs)
    return outs[0].reshape(BATCH * SEQ, A_HD), outs[1:]


def _band_b_kernel(sink_ref, q_ref, kvp_ref, kvc_ref, o_ref, c_ref):
    j = pl.program_id(1)
    valid_p, valid_c = _band_masks(BLK)
    valid_p = jnp.logical_and(valid_p, j > 0)
    lo = lax.broadcasted_iota(jnp.int32, (BLK, LANES), 1) < HEAD_DIM

    def variants(x):
        return x.astype(BF16), pltpu.roll(x, HEAD_DIM, axis=1).astype(BF16)

    kp = variants(kvp_ref[:, :B_KV])
    kc = variants(kvc_ref[:, :B_KV])
    vp = variants(kvp_ref[:, B_KV:])
    vc = variants(kvc_ref[:, B_KV:])
    for pair in range(B_HEADS // 2):
        sl = slice(pair * LANES, (pair + 1) * LANES)
        qp = q_ref[:, sl]
        outs = []
        for e in range(2):
            hq = 2 * pair + e
            hk = hq // B_GROUP
            sel = 0 if e == hk else 1
            qm = jnp.where(lo if e == 0 else jnp.logical_not(lo), qp, jnp.zeros_like(qp))
            sp = jnp.where(valid_p, _dot_nt(qm, kp[sel]), NEG_INF)
            sc = jnp.where(valid_c, _dot_nt(qm, kc[sel]), NEG_INF)
            sink = sink_ref[hq]
            m = jnp.maximum(jnp.max(sp, axis=-1, keepdims=True), jnp.max(sc, axis=-1, keepdims=True))
            m = jnp.maximum(m, sink)
            pp = jnp.exp(sp - m)
            pc = jnp.exp(sc - m)
            den = (jnp.sum(pp, axis=-1, keepdims=True) + jnp.sum(pc, axis=-1, keepdims=True)
                   + jnp.exp(sink - m))
            pv = (jnp.dot(pp.astype(BF16), vp[sel], preferred_element_type=F32)
                  + jnp.dot(pc.astype(BF16), vc[sel], preferred_element_type=F32))
            outs.append(pv / den)
        o_ref[:, sl] = jnp.where(lo, outs[0], outs[1]).astype(BF16)

    @pl.when(j == pl.num_programs(1) - 1)
    def _():
        c_ref[...] = kvc_ref[...].T


def _band_b(qb, kvb, sink):
    nb = SEQ // BLK
    qv = qb.reshape(BATCH, SEQ, B_Q)
    kvv = kvb.reshape(BATCH, SEQ, 2 * B_KV)
    prev = lambda j: jnp.maximum(j - 1, 0)
    o, c = pl.pallas_call(
        _band_b_kernel,
        grid=(BATCH, nb),
        in_specs=[
            pl.BlockSpec(memory_space=pltpu.SMEM),
            pl.BlockSpec((None, BLK, B_Q), lambda b, j: (b, j, 0)),
            pl.BlockSpec((None, BLK, 2 * B_KV), lambda b, j: (b, prev(j), 0)),
            pl.BlockSpec((None, BLK, 2 * B_KV), lambda b, j: (b, j, 0)),
        ],
        out_specs=[
            pl.BlockSpec((None, BLK, B_Q), lambda b, j: (b, j, 0)),
            pl.BlockSpec((None, 2 * B_KV, BLK), lambda b, j: (b, 0, 0)),
        ],
        out_shape=[
            jax.ShapeDtypeStruct((BATCH, SEQ, B_Q), BF16),
            jax.ShapeDtypeStruct((BATCH, 2 * B_KV, BLK), F32),
        ],
        compiler_params=_cparams("parallel", "arbitrary"),
        name="band_b",
    )(sink, qv, kvv, kvv)
    return o.reshape(BATCH * SEQ, B_Q), c


_COL_QA, _COL_KA, _COL_VA = 0, N_DIL * A_HEADS, 2 * N_DIL * A_HEADS
_COL_QB = 3 * N_DIL * A_HEADS
_COL_KB = _COL_QB + B_HEADS
_COL_VB = _COL_KB + B_KV_HEADS
_N_COLS = _COL_VB + B_KV_HEADS


def _sample_attn_kernel(sink_ref, cols_ref, c0_ref, c1_ref, c2_ref, cb_ref, b0_ref, b1_ref, b2_ref, bnew_ref, o_ref):
    caches = (c0_ref, c1_ref, c2_ref)
    biases = (b0_ref, b1_ref, b2_ref)
    lane = lax.broadcasted_iota(jnp.int32, (HEAD_DIM, LANES), 1)
    col = lambda c: cols_ref[:, c:c + 1]

    def head(q, k_new, v_new, kt, vt, bias, bias_new, sink):
        s = jnp.sum(kt * q, axis=0, keepdims=True)
        if bias is not None:
            s = jnp.where(bias > 0.5 * NEG_INF, s + bias, NEG_INF)
        s_new = jnp.sum(q * k_new, axis=0, keepdims=True)
        if bias_new is not None:
            s_new = s_new + bias_new
        m = jnp.maximum(jnp.max(s, axis=-1, keepdims=True), s_new)
        if sink is not None:
            m = jnp.maximum(m, sink)
        p = jnp.exp(s - m)
        p_new = jnp.exp(s_new - m)
        den = jnp.sum(p, axis=-1, keepdims=True) + p_new
        if sink is not None:
            den = den + jnp.exp(sink - m)
        pv = jnp.sum(vt * p, axis=-1, keepdims=True) + p_new * v_new
        return pv / den, m + jnp.log(den)

    out = jnp.zeros((HEAD_DIM, LANES), F32)
    for h in range(A_HEADS):
        os_, ls_ = [], []
        for g in range(N_DIL):
            c = g * A_HEADS + h
            o, lse = head(col(_COL_QA + c), col(_COL_KA + c), col(_COL_VA + c),
                          caches[g][0, h], caches[g][1, h],
                          biases[g][h:h + 1, :], bnew_ref[h:h + 1, g:g + 1], None)
            os_.append(o)
            ls_.append(lse)
        mx = jnp.maximum(jnp.maximum(ls_[0], ls_[1]), ls_[2])
        es = [jnp.exp(l - mx) for l in ls_]
        comb = (es[0] * os_[0] + es[1] * os_[1] + es[2] * os_[2]) / (es[0] + es[1] + es[2])
        out = jnp.where(lane == h, comb, out)
    for hq in range(B_HEADS):
        hk = hq // B_GROUP
        o, _ = head(col(_COL_QB + hq), col(_COL_KB + hk), col(_COL_VB + hk),
                    cb_ref[0, hk], cb_ref[1, hk], None, None, sink_ref[hq])
        out = jnp.where(lane == A_HEADS + hq, o, out)
    o_ref[...] = out


def _sample_attn(h_a, qb, kvb, caches_t, l, bias_lanes, bias_new, sink):
    n = DEC_BATCH
    ha = h_a.reshape(n, N_DIL, 3, A_HEADS, HEAD_DIM)
    kb = kvb.reshape(n, 2, B_KV_HEADS, HEAD_DIM)
    cols = jnp.concatenate([
        ha[:, :, 0].reshape(n, -1, HEAD_DIM), ha[:, :, 1].reshape(n, -1, HEAD_DIM), ha[:, :, 2].reshape(n, -1, HEAD_DIM),
        qb.astype(F32).reshape(n, B_HEADS, HEAD_DIM), kb[:, 0], kb[:, 1]], axis=1)
    cols = cols.transpose(0, 2, 1)
    cache_spec = lambda c: pl.BlockSpec((None, None) + c.shape[2:], lambda b: (l, b, 0, 0, 0, 0))
    full = lambda a: pl.BlockSpec(a.shape, lambda b: (0,) * a.ndim)
    out = pl.pallas_call(
        _sample_attn_kernel,
        grid=(n,),
        in_specs=[pl.BlockSpec(memory_space=pltpu.SMEM),
                  pl.BlockSpec((None, HEAD_DIM, _N_COLS), lambda b: (b, 0, 0))]
        + [cache_spec(c) for c in caches_t] + [full(a) for a in bias_lanes] + [full(bias_new)],
        out_specs=pl.BlockSpec((None, HEAD_DIM, LANES), lambda b: (b, 0, 0)),
        out_shape=jax.ShapeDtypeStruct((n, HEAD_DIM, LANES), F32),
        compiler_params=_cparams("parallel"),
        name="sample_attn",
    )(sink, cols, *caches_t, *bias_lanes, bias_new)
    heads = out.transpose(0, 2, 1)
    out_a = heads[:, :A_HEADS].reshape(n, A_HD)
    out_b = heads[:, A_HEADS:A_HEADS + B_HEADS].reshape(n, B_Q)
    return out_a.astype(BF16), out_b.astype(BF16)


def _post_attn_kernel(a_ref, ob_ref, gate_ref, x_ref, mod_ref, wpa_ref, wpb_ref, wo_ref, lng_ref, lnb_ref, o_ref):
    ya = jnp.dot(a_ref[...], wpa_ref[...], preferred_element_type=F32)
    yb = jnp.dot(ob_ref[...], wpb_ref[...], preferred_element_type=F32)
    mix = gate_ref[:, :D_MODEL] * ya + gate_ref[:, D_MODEL:] * yb
    y = jnp.dot(mix.astype(BF16), wo_ref[...], preferred_element_type=F32)
    z = ALPHA * x_ref[...] + mod_ref[2] * y
    o_ref[...] = _layer_norm(z, lng_ref[0:1, :], lnb_ref[0:1, :])


def _post_attn(oa, ob, gates, x, mod, w_pa, w_pb, w_o, ln_g, ln_b, tm, rows_per_mod):
    t = x.shape[0]
    row = lambda w: pl.BlockSpec((tm, w), lambda i: (i, 0))
    const = lambda shape: pl.BlockSpec(shape, lambda i: (0, 0), pipeline_mode=pl.Buffered(1))
    return pl.pallas_call(
        _post_attn_kernel,
        grid=(t // tm,),
        in_specs=[
            row(A_HD), row(B_Q), row(G_COLS), row(D_MODEL),
            _mod_spec(mod, tm, rows_per_mod),
            const((A_HD, D_MODEL)), const((B_Q, D_MODEL)), const((D_MODEL, D_MODEL)),
            const((2, D_MODEL)), const((2, D_MODEL)),
        ],
        out_specs=row(D_MODEL),
        out_shape=jax.ShapeDtypeStruct((t, D_MODEL), F32),
        compiler_params=_cparams("parallel"),
        name="post_attn",
    )(oa, ob, gates, x, mod, w_pa, w_pb, w_o, ln_g, ln_b)


def _ffn_kernel(x_ref, mod_ref, wg_ref, wu_ref, wd_ref, lng_ref, lnb_ref, o_ref, u_ref, acc_ref):
    k = pl.program_id(1)

    @pl.when(k == 0)
    def _():
        u_ref[...] = _modulate(x_ref, mod_ref, 3, 4)
        acc_ref[...] = jnp.zeros_like(acc_ref)

    u = u_ref[...]
    gate = jnp.dot(u, wg_ref[...], preferred_element_type=F32)
    up = jnp.dot(u, wu_ref[...], preferred_element_type=F32)
    act = (gate * jax.nn.sigmoid(gate) * up).astype(BF16)
    acc_ref[...] += jnp.dot(act, wd_ref[...], preferred_element_type=F32)

    @pl.when(k == pl.num_programs(1) - 1)
    def _():
        z = ALPHA * x_ref[...] + mod_ref[5] * acc_ref[...]
        o_ref[...] = _layer_norm(z, lng_ref[1:2, :], lnb_ref[1:2, :])


def _ffn(x, mod, w_gu, w_down, ln_g, ln_b, tm, rows_per_mod):
    t = x.shape[0]
    tf = 512
    nf = D_FF // tf
    return pl.pallas_call(
        _ffn_kernel,
        grid=(t // tm, nf),
        in_specs=[
            pl.BlockSpec((tm, D_MODEL), lambda i, k: (i, 0)),
            _mod_spec(mod, tm, rows_per_mod),
            pl.BlockSpec((D_MODEL, tf), lambda i, k: (0, k)),
            pl.BlockSpec((D_MODEL, tf), lambda i, k: (0, k + nf)),
            pl.BlockSpec((tf, D_MODEL), lambda i, k: (k, 0)),
            pl.BlockSpec((2, D_MODEL), lambda i, k: (0, 0)),
            pl.BlockSpec((2, D_MODEL), lambda i, k: (0, 0)),
        ],
        out_specs=pl.BlockSpec((tm, D_MODEL), lambda i, k: (i, 0)),
        out_shape=jax.ShapeDtypeStruct((t, D_MODEL), F32),
        scratch_shapes=[pltpu.VMEM((tm, D_MODEL), BF16), pltpu.VMEM((tm, D_MODEL), F32)],
        compiler_params=_cparams("parallel", "arbitrary"),
        name="ffn",
    )(x, mod, w_gu, w_gu, w_down, ln_g, ln_b)


def _t5_bucket(dist):
    exact = N_BUCKETS // 2
    n = jnp.maximum(dist, 0)
    log_ratio = jnp.log(jnp.maximum(n, exact).astype(F32) / exact) / math.log(T5_MAX_DIST / exact)
    large = jnp.minimum(exact + (log_ratio * (N_BUCKETS - exact)).astype(jnp.int32), N_BUCKETS - 1)
    return jnp.where(n < exact, n, large)


def _bias_tables(t5_table):
    tps, tcs, lanes, news = [], [], [], []
    for g, (w, d) in enumerate(DILATIONS):
        bucket = _t5_bucket(d * jnp.arange(BLK + 1))
        onehot = (bucket[:, None] == jnp.arange(N_BUCKETS)[None, :]).astype(F32)
        bias = jnp.dot(onehot, t5_table[:, g * A_HEADS:(g + 1) * A_HEADS],
                       precision=lax.Precision.HIGHEST).T
        wv = jnp.concatenate([bias[:, ::-1], jnp.zeros((A_HEADS, BLK), F32)], axis=1)
        toep = jnp.tile(wv, (1, BLK))[:, :BLK * 2 * BLK].reshape(A_HEADS, BLK, 2 * BLK)
        tps.append(toep[:, :, :BLK])
        tcs.append(toep[:, :, BLK:])
        by_row = jnp.repeat(bias[:, BLK:0:-1], d, axis=1)
        t = jnp.arange(BLK * d)[None, :]
        lanes.append(jnp.where(t % d == 0, by_row, NEG_INF))
        news.append(bias[:, 0])
    new = jnp.zeros((A_HEADS, LANES), F32).at[:, :N_DIL].set(jnp.stack(news, axis=1))
    return jnp.stack(tps), jnp.stack(tcs), lanes, new


def _rope_tables(pos):
    half = HEAD_DIM // 2
    inv = ROPE_THETA ** (-jnp.arange(half, dtype=F32) / half)
    ang = pos.astype(F32)[:, None] * inv[None]
    cos, sin = jnp.cos(ang), jnp.sin(ang)
    cos_t = jnp.concatenate([cos, cos, cos, cos], axis=-1)
    sin_t = jnp.concatenate([-sin, sin, -sin, sin], axis=-1)
    return cos_t, sin_t


def kernel(x_prompt, x_sample, c_prompt, c_sample, cache_a0, cache_a1, cache_a2, cache_b, t5_table, w_ada, b_ada,
           w_in, sinks, w_pa, w_pb, w_o, w_gu, w_down, ln_g, ln_b):
    tp_rows = BATCH * SEQ
    mod = _adaln(jnp.concatenate([c_prompt, c_sample], axis=0), w_ada, b_ada)
    mod_p = mod[:, :BATCH].reshape(DEPTH, BATCH, 6, 1, D_MODEL).transpose(0, 2, 1, 3, 4)
    mod_s = mod[:, BATCH:].reshape(DEPTH, 1, DEC_BATCH, 6, D_MODEL).transpose(0, 3, 1, 2, 4)

    q_scale = jnp.concatenate([jnp.full((A_HD,), Q_SCALE, F32), jnp.ones((2 * A_HD,), F32)])
    w_a = (w_in[:, :, :A_COLS] * jnp.tile(q_scale, N_DIL)).astype(BF16)
    b_scale = jnp.concatenate([jnp.full((B_Q,), Q_SCALE, F32), jnp.ones((2 * B_KV,), F32)])
    w_b = (w_in[:, :, A_COLS:A_COLS + B_COLS] * b_scale).astype(BF16)
    w_g = w_in[:, :, A_COLS + B_COLS:].astype(BF16)
    w_pa_h, w_pb_h, w_o_h = w_pa.astype(BF16), w_pb.astype(BF16), w_o.astype(BF16)
    w_gu_h, w_down_h = w_gu.astype(BF16), w_down.astype(BF16)

    tps, tcs, bias_lanes, bias_new = _bias_tables(t5_table)
    cos_p, sin_p = _rope_tables(jnp.arange(SEQ))
    cos_s, sin_s = _rope_tables(jnp.full((DEC_BATCH,), PAST_LEN))
    caches_t = [c.transpose(0, 1, 3, 4, 5, 2) for c in (cache_a0, cache_a1, cache_a2, cache_b)]

    xp = x_prompt.reshape(tp_rows, D_MODEL)
    xs = x_sample.reshape(DEC_BATCH, D_MODEL)
    rows_a_p = [[] for _ in range(N_DIL)]
    rows_a_s = [[] for _ in range(N_DIL)]
    rows_b_p, rows_b_s = [], []
    for l in range(DEPTH):
        sink = sinks[l].reshape(B_HEADS)

        h_a = _proj_a(xp, mod_p[l], w_a[l], 512, SEQ)
        qb, kvb = _proj_b(xp, mod_p[l], w_b[l], cos_p, sin_p, 512, SEQ)
        gates = _gates(xp, mod_p[l], w_g[l], 512, SEQ)
        oa, kv_rows = _dil_attn(h_a, tps, tcs)
        ob, b_rows = _band_b(qb, kvb, sink)
        xp = _post_attn(oa, ob, gates, xp, mod_p[l], w_pa_h[l], w_pb_h[l], w_o_h[l], ln_g[l], ln_b[l], 256, SEQ)
        xp = _ffn(xp, mod_p[l], w_gu_h[l], w_down_h[l], ln_g[l], ln_b[l], 512, SEQ)
        for g in range(N_DIL):
            rows_a_p[g].append(kv_rows[g])
        rows_b_p.append(b_rows.reshape(BATCH, 2, B_KV_HEADS, HEAD_DIM, BLK))

        h_a = _proj_a(xs, mod_s[l], w_a[l], DEC_BATCH, DEC_BATCH)
        qb, kvb = _proj_b(xs, mod_s[l], w_b[l], cos_s, sin_s, DEC_BATCH, DEC_BATCH)
        gates = _gates(xs, mod_s[l], w_g[l], DEC_BATCH, DEC_BATCH)
        oa, ob = _sample_attn(h_a, qb, kvb, caches_t, l, bias_lanes, bias_new, sink)
        xs = _post_attn(oa, ob, gates, xs, mod_s[l], w_pa_h[l], w_pb_h[l], w_o_h[l], ln_g[l], ln_b[l],
                        DEC_BATCH, DEC_BATCH)
        xs = _ffn(xs, mod_s[l], w_gu_h[l], w_down_h[l], ln_g[l], ln_b[l], DEC_BATCH, DEC_BATCH)
        ha = h_a.reshape(DEC_BATCH, 1, N_DIL, 3, A_HEADS, HEAD_DIM)
        for g in range(N_DIL):
            rows_a_s[g].append(ha[:, :, g, 1:])
        rows_b_s.append(kvb.reshape(DEC_BATCH, 1, 2, B_KV_HEADS, HEAD_DIM))

    to_rows_major = lambda parts: jnp.stack(parts).transpose(0, 1, 5, 2, 3, 4)
    outs = [xp.reshape(BATCH, SEQ, D_MODEL), xs.reshape(DEC_BATCH, 1, D_MODEL)]
    for g in range(N_DIL):
        outs.append(to_rows_major(rows_a_p[g]))
        outs.append(jnp.stack(rows_a_s[g]))
    outs.append(to_rows_major(rows_b_p))
    outs.append(jnp.stack(rows_b_s))
    return tuple(outs)
```

```python
import math

import jax
import jax.numpy as jnp
from jax import lax
from jax.experimental import pallas as pl
from jax.experimental.pallas import tpu as pltpu

F32 = jnp.float32
BF16 = jnp.bfloat16

D_MODEL = 2048
BATCH = 8
SEQ = 2048
DEPTH = 2
DEC_BATCH = 128
PAST_LEN = 8192
HEAD_DIM = 64
DILATIONS = ((128, 1), (512, 4), (2048, 16))
N_DIL = 3
A_HEADS = 8
B_HEADS = 16
B_KV_HEADS = 2
B_GROUP = B_HEADS // B_KV_HEADS
B_WINDOW = 128
ROPE_THETA = 150000.0
N_BUCKETS = 32
T5_MAX_DIST = 2048
D_FF = ((8 * D_MODEL + 3 * 256 - 1) // (3 * 256)) * 256
ALPHA = (2 * DEPTH) ** 0.25
LN_EPS = 1e-5
NEG_INF = -1e30

BLK = 128
A_HD = A_HEADS * HEAD_DIM
A_GROUP_COLS = 3 * A_HD
A_COLS = N_DIL * A_GROUP_COLS
B_Q = B_HEADS * HEAD_DIM
B_KV = B_KV_HEADS * HEAD_DIM
B_COLS = B_Q + 2 * B_KV
G_COLS = 2 * D_MODEL
LANES = 128
PAIRS_PER_STEP = 2
UNIT_UNROLL = 4
Q_SCALE = HEAD_DIM ** -0.5
A_KEEP = tuple(min(w, SEQ) for w, _ in DILATIONS)

_VMEM_LIMIT = 56 * 1024 * 1024


def _cparams(*sem):
    return pltpu.CompilerParams(dimension_semantics=sem, vmem_limit_bytes=_VMEM_LIMIT)


def _mod_spec(mod, tm, rows_per_mod):
    mr = mod.shape[2]
    return pl.BlockSpec((6, None, mr, D_MODEL), lambda i, *_: (0, (i * tm) // rows_per_mod, 0, 0))


def _layer_norm(z, g, b):
    mu = jnp.mean(z, axis=-1, keepdims=True)
    zc = z - mu
    var = jnp.mean(zc * zc, axis=-1, keepdims=True)
    return zc * lax.rsqrt(var + LN_EPS) * g + b


def _adaln_kernel(c_ref, w_ref, b_ref, o_ref):
    c = c_ref[...]
    s = (c * jax.nn.sigmoid(c)).astype(BF16)
    o_ref[...] = jnp.dot(s, w_ref[...].astype(BF16), preferred_element_type=F32) + b_ref[...]


def _adaln(c_all, w_ada, b_ada):
    rows = c_all.shape[0]
    tn = 1024
    return pl.pallas_call(
        _adaln_kernel,
        grid=(DEPTH, 6 * D_MODEL // tn),
        in_specs=[
            pl.BlockSpec((rows, D_MODEL), lambda l, j: (0, 0)),
            pl.BlockSpec((None, D_MODEL, tn), lambda l, j: (l, 0, j)),
            pl.BlockSpec((None, 1, tn), lambda l, j: (l, 0, j)),
        ],
        out_specs=pl.BlockSpec((None, rows, tn), lambda l, j: (l, 0, j)),
        out_shape=jax.ShapeDtypeStruct((DEPTH, rows, 6 * D_MODEL), F32),
        compiler_params=_cparams("parallel", "parallel"),
        name="adaln",
    )(c_all, w_ada, b_ada.reshape(DEPTH, 1, 6 * D_MODEL))


def _modulate(x_ref, mod_ref, shift_idx, scale_idx):
    return (x_ref[...] * (1.0 + mod_ref[scale_idx]) + mod_ref[shift_idx]).astype(BF16)


def _proj_a_kernel(x_ref, mod_ref, w_ref, o_ref, u_ref):
    @pl.when(pl.program_id(1) == 0)
    def _():
        u_ref[...] = _modulate(x_ref, mod_ref, 0, 1)

    o_ref[...] = jnp.dot(u_ref[...], w_ref[...], preferred_element_type=F32)


def _proj_a(x, mod, w_a, l, tm, rows_per_mod):
    t = x.shape[0]
    return pl.pallas_call(
        _proj_a_kernel,
        grid=(t // tm, N_DIL),
        in_specs=[
            pl.BlockSpec((tm, D_MODEL), lambda i, g: (i, 0)),
            _mod_spec(mod, tm, rows_per_mod),
            pl.BlockSpec((None, D_MODEL, A_GROUP_COLS), lambda i, g: (l, 0, g)),
        ],
        out_specs=pl.BlockSpec((tm, A_GROUP_COLS), lambda i, g: (i, g)),
        out_shape=jax.ShapeDtypeStruct((t, A_COLS), F32),
        scratch_shapes=[pltpu.VMEM((tm, D_MODEL), BF16)],
        compiler_params=_cparams("parallel", "arbitrary"),
        name="proj_a",
    )(x, mod, w_a)


def _proj_b_kernel(x_ref, mod_ref, w_ref, cos_ref, sin_ref, q_ref, kv_ref):
    u = _modulate(x_ref, mod_ref, 0, 1)
    acc = jnp.dot(u, w_ref[...], preferred_element_type=F32)
    cos = cos_ref[...]
    sin = sin_ref[...]
    lane = lax.broadcasted_iota(jnp.int32, cos.shape, 1)
    first_half = (lane % HEAD_DIM) < (HEAD_DIM // 2)

    def rope(v):
        partner = jnp.where(first_half,
                            pltpu.roll(v, LANES - HEAD_DIM // 2, axis=1),
                            pltpu.roll(v, HEAD_DIM // 2, axis=1))
        return v * cos + partner * sin

    for c in range(B_Q // LANES):
        q_ref[:, c * LANES:(c + 1) * LANES] = rope(acc[:, c * LANES:(c + 1) * LANES]).astype(BF16)
    kv_ref[:, :B_KV] = rope(acc[:, B_Q:B_Q + B_KV])
    kv_ref[:, B_KV:] = acc[:, B_Q + B_KV:]


def _proj_b(x, mod, w_b, l, cos_t, sin_t, tm, rows_per_mod):
    t = x.shape[0]
    n_pos_blocks = cos_t.shape[0] // tm
    return pl.pallas_call(
        _proj_b_kernel,
        grid=(t // tm,),
        in_specs=[
            pl.BlockSpec((tm, D_MODEL), lambda i: (i, 0)),
            _mod_spec(mod, tm, rows_per_mod),
            pl.BlockSpec((None, D_MODEL, B_COLS), lambda i: (l, 0, 0)),
            pl.BlockSpec((tm, LANES), lambda i: (i % n_pos_blocks, 0)),
            pl.BlockSpec((tm, LANES), lambda i: (i % n_pos_blocks, 0)),
        ],
        out_specs=[
            pl.BlockSpec((tm, B_Q), lambda i: (i, 0)),
            pl.BlockSpec((tm, 2 * B_KV), lambda i: (i, 0)),
        ],
        out_shape=[
            jax.ShapeDtypeStruct((t, B_Q), BF16),
            jax.ShapeDtypeStruct((t, 2 * B_KV), F32),
        ],
        compiler_params=_cparams("parallel"),
        name="proj_b",
    )(x, mod, w_b, cos_t, sin_t)


def _gates_kernel(x_ref, mod_ref, w_ref, o_ref, u_ref):
    @pl.when(pl.program_id(1) == 0)
    def _():
        u_ref[...] = _modulate(x_ref, mod_ref, 0, 1)

    acc = jnp.dot(u_ref[...], w_ref[...], preferred_element_type=F32)
    o_ref[...] = jax.nn.sigmoid(acc)


def _gates(x, mod, w_g, l, tm, rows_per_mod):
    t = x.shape[0]
    tn = 1024
    return pl.pallas_call(
        _gates_kernel,
        grid=(t // tm, G_COLS // tn),
        in_specs=[
            pl.BlockSpec((tm, D_MODEL), lambda i, j: (i, 0)),
            _mod_spec(mod, tm, rows_per_mod),
            pl.BlockSpec((None, D_MODEL, tn), lambda i, j: (l, 0, j)),
        ],
        out_specs=pl.BlockSpec((tm, tn), lambda i, j: (i, j)),
        out_shape=jax.ShapeDtypeStruct((t, G_COLS), F32),
        scratch_shapes=[pltpu.VMEM((tm, D_MODEL), BF16)],
        compiler_params=_cparams("parallel", "arbitrary"),
        name="gates",
    )(x, mod, w_g)


def _dot_nt(a, b):
    return lax.dot_general(a, b, (((1,), (1,)), ((), ())), preferred_element_type=F32)


def _band_masks():
    row = lax.broadcasted_iota(jnp.int32, (2 * BLK, BLK), 0) % BLK
    col = lax.broadcasted_iota(jnp.int32, (2 * BLK, BLK), 1)
    return col >= row, col <= row


def _rows(start, d):
    return pl.ds(start, BLK, stride=d) if d > 1 else pl.ds(pl.multiple_of(start, BLK), BLK)


def _stack_pair(q):
    lo = lax.broadcasted_iota(jnp.int32, (BLK, LANES), 1) < HEAD_DIM
    zero = jnp.zeros_like(q)
    return jnp.concatenate([jnp.where(lo, q, zero), jnp.where(lo, zero, q)], axis=0)


def _unstack_pair(x):
    lo = lax.broadcasted_iota(jnp.int32, (BLK, LANES), 1) < HEAD_DIM
    return jnp.where(lo, jnp.broadcast_to(x[:BLK], (BLK, LANES)), jnp.broadcast_to(x[BLK:], (BLK, LANES)))


def _pair_attend(qs, cur, prev, has_prev, sink):
    valid_p, valid_c = _band_masks()

    def scores(k, bias, valid):
        s = _dot_nt(qs, k)
        if bias is not None:
            s = s + bias
        return jnp.where(valid, s, NEG_INF)

    sc = scores(cur[0], cur[2], valid_c)
    if prev is not None:
        sp = jnp.where(has_prev, scores(prev[0], prev[2], valid_p), NEG_INF)
        m = jnp.max(jnp.maximum(sp, sc), axis=-1, keepdims=True)
    else:
        m = jnp.max(sc, axis=-1, keepdims=True)
    if sink is not None:
        m = jnp.maximum(m, sink)
    pc = jnp.exp(sc - m)
    pv = jnp.dot(pc.astype(BF16), cur[1], preferred_element_type=F32)
    if prev is not None:
        pp = jnp.exp(sp - m)
        pv = pv + jnp.dot(pp.astype(BF16), prev[1], preferred_element_type=F32)
        den = jnp.sum(pp + pc, axis=-1, keepdims=True)
    else:
        den = jnp.sum(pc, axis=-1, keepdims=True)
    if sink is not None:
        den = den + jnp.exp(sink - m)
    return _unstack_pair(pv), _unstack_pair(m), _unstack_pair(den)


def _dil_group(gi, q_refs, k_refs, v_refs, tp_ref, tc_ref, acc_ref, m_ref, l_ref):
    d = DILATIONS[gi][1]
    nblk = SEQ // (BLK * d)

    def unit(u, carry):
        r = u // nblk
        j = u % nblk
        start = j * (BLK * d) + r
        start_prev = jnp.maximum(j - 1, 0) * (BLK * d) + r
        rows = _rows(start, d)
        rows_prev = _rows(start_prev, d)
        for p in range(PAIRS_PER_STEP):
            qs = _stack_pair(q_refs[p][rows, :].astype(BF16))
            cur = (k_refs[p][rows, :].astype(BF16), v_refs[p][rows, :].astype(BF16),
                   tc_ref[2 * p:2 * p + 2].reshape(2 * BLK, BLK))
            prev = None
            if nblk > 1:
                prev = (k_refs[p][rows_prev, :].astype(BF16), v_refs[p][rows_prev, :].astype(BF16),
                        tp_ref[2 * p:2 * p + 2].reshape(2 * BLK, BLK))
            num, m_g, l_g = _pair_attend(qs, cur, prev, j > 0, None)
            if gi == 0:
                acc_ref[p, rows, :] = num
                m_ref[p, rows, :] = m_g
                l_ref[p, rows, :] = l_g
            else:
                m_old = m_ref[p, rows, :]
                m_new = jnp.maximum(m_old, m_g)
                a = jnp.exp(m_old - m_new)
                b = jnp.exp(m_g - m_new)
                acc_ref[p, rows, :] = acc_ref[p, rows, :] * a + num * b
                l_ref[p, rows, :] = l_ref[p, rows, :] * a + l_g * b
                m_ref[p, rows, :] = m_new
        return carry

    lax.fori_loop(0, SEQ // BLK, unit, 0, unroll=UNIT_UNROLL)


def _dil_attn_kernel(q0, q1, k0, k1, v0, v1, tp_ref, tc_ref, o_ref, c0_ref, c1_ref, c2_ref, acc_ref, m_ref, l_ref):
    g = pl.program_id(2)
    q_refs, k_refs, v_refs = (q0, q1), (k0, k1), (v0, v1)
    c_refs = (c0_ref, c1_ref, c2_ref)
    for gi in range(N_DIL):
        @pl.when(g == gi)
        def _(gi=gi):
            _dil_group(gi, q_refs, k_refs, v_refs, tp_ref, tc_ref, acc_ref, m_ref, l_ref)
            keep = A_KEEP[gi]
            for p in range(PAIRS_PER_STEP):
                c_refs[gi][0, 2 * p:2 * p + 2] = k_refs[p][SEQ - keep:, :].T.reshape(2, HEAD_DIM, keep)
                c_refs[gi][1, 2 * p:2 * p + 2] = v_refs[p][SEQ - keep:, :].T.reshape(2, HEAD_DIM, keep)

    @pl.when(g == N_DIL - 1)
    def _():
        for p in range(PAIRS_PER_STEP):
            o_ref[:, p * LANES:(p + 1) * LANES] = (acc_ref[p] / l_ref[p]).astype(BF16)


def _dil_attn(h_a, tps, tcs):
    hv = h_a.reshape(BATCH, SEQ, A_COLS)
    steps = A_HEADS // 2 // PAIRS_PER_STEP
    heads = 2 * PAIRS_PER_STEP
    slab = (None, SEQ, LANES)
    col = lambda part, p: (lambda b, s, g: (b, 0, g * 12 + part * 4 + s * PAIRS_PER_STEP + p))
    tbl = pl.BlockSpec((None, heads, BLK, BLK), lambda b, s, g: (g, s, 0, 0))
    cache = lambda keep: pl.BlockSpec((None, 2, heads, HEAD_DIM, keep), lambda b, s, g: (b, 0, s, 0, 0))
    outs = pl.pallas_call(
        _dil_attn_kernel,
        grid=(BATCH, steps, N_DIL),
        in_specs=[pl.BlockSpec(slab, col(part, p)) for part in range(3) for p in range(PAIRS_PER_STEP)] + [tbl, tbl],
        out_specs=[pl.BlockSpec((None, SEQ, heads * HEAD_DIM), lambda b, s, g: (b, 0, s))]
        + [cache(keep) for keep in A_KEEP],
        out_shape=[jax.ShapeDtypeStruct((BATCH, SEQ, A_HD), BF16)]
        + [jax.ShapeDtypeStruct((BATCH, 2, A_HEADS, HEAD_DIM, keep), F32) for keep in A_KEEP],
        scratch_shapes=[pltpu.VMEM((PAIRS_PER_STEP, SEQ, LANES), F32)] * 3,
        compiler_params=_cparams("parallel", "parallel", "arbitrary"),
        name="dil_attn",
    )(*([hv] * (3 * PAIRS_PER_STEP)), tps, tcs)
    return outs[0].reshape(BATCH * SEQ, A_HD), outs[1:]


def _band_b_kernel(sink_ref, q_ref, kvp_ref, kvc_ref, o_ref, c_ref):
    j = pl.program_id(1)
    lo = lax.broadcasted_iota(jnp.int32, (BLK, LANES), 1) < HEAD_DIM
    first = lax.broadcasted_iota(jnp.int32, (2 * BLK, 1), 0) < BLK

    def per_kv_head(x):
        rolled = pltpu.roll(x, HEAD_DIM, axis=1)
        return jnp.where(lo, x, rolled).astype(BF16), jnp.where(lo, rolled, x).astype(BF16)

    kp, kc = per_kv_head(kvp_ref[:, :B_KV]), per_kv_head(kvc_ref[:, :B_KV])
    vp, vc = per_kv_head(kvp_ref[:, B_KV:]), per_kv_head(kvc_ref[:, B_KV:])
    for pair in range(B_HEADS // 2):
        hk = (2 * pair) // B_GROUP
        sl = slice(pair * LANES, (pair + 1) * LANES)
        sink = jnp.where(first, sink_ref[2 * pair], sink_ref[2 * pair + 1])
        num, _, den = _pair_attend(_stack_pair(q_ref[:, sl]), (kc[hk], vc[hk], None), (kp[hk], vp[hk], None),
                                   j > 0, sink)
        o_ref[:, sl] = (num / den).astype(BF16)

    @pl.when(j == pl.num_programs(1) - 1)
    def _():
        c_ref[...] = kvc_ref[...].T


def _band_b(qb, kvb, sink):
    nb = SEQ // BLK
    qv = qb.reshape(BATCH, SEQ, B_Q)
    kvv = kvb.reshape(BATCH, SEQ, 2 * B_KV)
    prev = lambda j: jnp.maximum(j - 1, 0)
    o, c = pl.pallas_call(
        _band_b_kernel,
        grid=(BATCH, nb),
        in_specs=[
            pl.BlockSpec(memory_space=pltpu.SMEM),
            pl.BlockSpec((None, BLK, B_Q), lambda b, j: (b, j, 0)),
            pl.BlockSpec((None, BLK, 2 * B_KV), lambda b, j: (b, prev(j), 0)),
            pl.BlockSpec((None, BLK, 2 * B_KV), lambda b, j: (b, j, 0)),
        ],
        out_specs=[
            pl.BlockSpec((None, BLK, B_Q), lambda b, j: (b, j, 0)),
            pl.BlockSpec((None, 2 * B_KV, BLK), lambda b, j: (b, 0, 0)),
        ],
        out_shape=[
            jax.ShapeDtypeStruct((BATCH, SEQ, B_Q), BF16),
            jax.ShapeDtypeStruct((BATCH, 2 * B_KV, BLK), F32),
        ],
        compiler_params=_cparams("parallel", "arbitrary"),
        name="band_b",
    )(sink, qv, kvv, kvv)
    return o.reshape(BATCH * SEQ, B_Q), c


_COL_QA, _COL_KA, _COL_VA = 0, N_DIL * A_HEADS, 2 * N_DIL * A_HEADS
_COL_QB = 3 * N_DIL * A_HEADS
_COL_KB = _COL_QB + B_HEADS
_COL_VB = _COL_KB + B_KV_HEADS
_N_COLS = _COL_VB + B_KV_HEADS

_LANE_CHUNK = 512


def _sample_attn_kernel(cols_ref, c0_ref, c1_ref, c2_ref, cb_ref, b0_ref, b1_ref, b2_ref, bnew_ref, sink_ref, o_ref):
    caches = (c0_ref, c1_ref, c2_ref)
    biases = (b0_ref, b1_ref, b2_ref)

    def cols(start, count):
        return jnp.stack([cols_ref[:, start + i:start + i + 1] for i in range(count)])

    def attend(q, k_new, v_new, kt_ref, vt_ref, bias_ref, bias_new, sink):
        n = kt_ref.shape[-1]
        chunk = min(n, _LANE_CHUNK)
        parts = []
        for c in range(n // chunk):
            sl = slice(c * chunk, (c + 1) * chunk)
            s = jnp.sum(kt_ref[:, :, sl] * q, axis=1, keepdims=True)
            if bias_ref is not None:
                bias = bias_ref[:, :, sl]
                s = jnp.where(bias > 0.5 * NEG_INF, s + bias, NEG_INF)
            parts.append(s)
        s_new = jnp.sum(q * k_new, axis=1, keepdims=True)
        if bias_new is not None:
            s_new = s_new + bias_new
        m = s_new
        for s in parts:
            m = jnp.maximum(m, jnp.max(s, axis=-1, keepdims=True))
        if sink is not None:
            m = jnp.maximum(m, sink)
        p_new = jnp.exp(s_new - m)
        den = p_new
        pv = p_new * v_new
        for c, s in enumerate(parts):
            p = jnp.exp(s - m)
            den = den + jnp.sum(p, axis=-1, keepdims=True)
            pv = pv + jnp.sum(vt_ref[:, :, c * chunk:(c + 1) * chunk] * p, axis=-1, keepdims=True)
        if sink is not None:
            den = den + jnp.exp(sink - m)
        return pv / den, m + jnp.log(den)

    os_, ls_ = [], []
    for g in range(N_DIL):
        c = g * A_HEADS
        o, lse = attend(cols(_COL_QA + c, A_HEADS), cols(_COL_KA + c, A_HEADS), cols(_COL_VA + c, A_HEADS),
                        caches[g].at[0], caches[g].at[1], biases[g], bnew_ref[:, :, g:g + 1], None)
        os_.append(o)
        ls_.append(lse)
    mx = jnp.maximum(jnp.maximum(ls_[0], ls_[1]), ls_[2])
    es = [jnp.exp(l - mx) for l in ls_]
    out_a = (es[0] * os_[0] + es[1] * os_[1] + es[2] * os_[2]) / (es[0] + es[1] + es[2])
    out_b = []
    for hk in range(B_KV_HEADS):
        o, _ = attend(cols(_COL_QB + hk * B_GROUP, B_GROUP), cols(_COL_KB + hk, 1), cols(_COL_VB + hk, 1),
                      cb_ref.at[0, hk:hk + 1], cb_ref.at[1, hk:hk + 1], None, None,
                      sink_ref[hk * B_GROUP:(hk + 1) * B_GROUP])
        out_b.append(o)

    lane = lax.broadcasted_iota(jnp.int32, (HEAD_DIM, LANES), 1)
    out = jnp.zeros((HEAD_DIM, LANES), F32)
    for h in range(A_HEADS):
        out = jnp.where(lane == h, out_a[h], out)
    for hq in range(B_HEADS):
        out = jnp.where(lane == A_HEADS + hq, out_b[hq // B_GROUP][hq % B_GROUP], out)
    o_ref[...] = out


def _sample_attn(h_a, qb, kvb, caches_t, l, bias_lanes, bias_new, sink):
    n = DEC_BATCH
    ha = h_a.reshape(n, N_DIL, 3, A_HEADS, HEAD_DIM)
    kb = kvb.reshape(n, 2, B_KV_HEADS, HEAD_DIM)
    cols = jnp.concatenate([
        ha[:, :, 0].reshape(n, -1, HEAD_DIM), ha[:, :, 1].reshape(n, -1, HEAD_DIM), ha[:, :, 2].reshape(n, -1, HEAD_DIM),
        qb.astype(F32).reshape(n, B_HEADS, HEAD_DIM), kb[:, 0], kb[:, 1]], axis=1)
    cols = cols.transpose(0, 2, 1)
    cache_spec = lambda c: pl.BlockSpec((None, None) + c.shape[2:], lambda b: (l, b, 0, 0, 0, 0))
    full = lambda a: pl.BlockSpec(a.shape, lambda b: (0,) * a.ndim)
    small = [a[:, None, :] for a in bias_lanes] + [bias_new[:, None, :], sink.reshape(B_HEADS, 1, 1)]
    out = pl.pallas_call(
        _sample_attn_kernel,
        grid=(n,),
        in_specs=[pl.BlockSpec((None, HEAD_DIM, _N_COLS), lambda b: (b, 0, 0))]
        + [cache_spec(c) for c in caches_t] + [full(a) for a in small],
        out_specs=pl.BlockSpec((None, HEAD_DIM, LANES), lambda b: (b, 0, 0)),
        out_shape=jax.ShapeDtypeStruct((n, HEAD_DIM, LANES), F32),
        compiler_params=_cparams("parallel"),
        name="sample_attn",
    )(cols, *caches_t, *small)
    heads = out.transpose(0, 2, 1)
    out_a = heads[:, :A_HEADS].reshape(n, A_HD)
    out_b = heads[:, A_HEADS:A_HEADS + B_HEADS].reshape(n, B_Q)
    return out_a.astype(BF16), out_b.astype(BF16)


def _post_attn_kernel(a_ref, ob_ref, gate_ref, x_ref, mod_ref, wpa_ref, wpb_ref, wo_ref, lng_ref, lnb_ref, o_ref):
    ya = jnp.dot(a_ref[...], wpa_ref[...], preferred_element_type=F32)
    yb = jnp.dot(ob_ref[...], wpb_ref[...], preferred_element_type=F32)
    mix = gate_ref[:, :D_MODEL] * ya + gate_ref[:, D_MODEL:] * yb
    y = jnp.dot(mix.astype(BF16), wo_ref[...], preferred_element_type=F32)
    z = ALPHA * x_ref[...] + mod_ref[2] * y
    o_ref[...] = _layer_norm(z, lng_ref[0:1, :], lnb_ref[0:1, :])


def _post_attn(oa, ob, gates, x, mod, w_pa, w_pb, w_o, ln_g, ln_b, l, tm, rows_per_mod):
    t = x.shape[0]
    row = lambda w: pl.BlockSpec((tm, w), lambda i: (i, 0))
    const = lambda shape: pl.BlockSpec((None,) + shape, lambda i: (l, 0, 0), pipeline_mode=pl.Buffered(1))
    return pl.pallas_call(
        _post_attn_kernel,
        grid=(t // tm,),
        in_specs=[
            row(A_HD), row(B_Q), row(G_COLS), row(D_MODEL),
            _mod_spec(mod, tm, rows_per_mod),
            const((A_HD, D_MODEL)), const((B_Q, D_MODEL)), const((D_MODEL, D_MODEL)),
            const((2, D_MODEL)), const((2, D_MODEL)),
        ],
        out_specs=row(D_MODEL),
        out_shape=jax.ShapeDtypeStruct((t, D_MODEL), F32),
        compiler_params=_cparams("parallel"),
        name="post_attn",
    )(oa, ob, gates, x, mod, w_pa, w_pb, w_o, ln_g, ln_b)


def _ffn_kernel(x_ref, mod_ref, wg_ref, wu_ref, wd_ref, lng_ref, lnb_ref, o_ref, u_ref, acc_ref):
    k = pl.program_id(1)

    @pl.when(k == 0)
    def _():
        u_ref[...] = _modulate(x_ref, mod_ref, 3, 4)
        acc_ref[...] = jnp.zeros_like(acc_ref)

    u = u_ref[...]
    gate = jnp.dot(u, wg_ref[...], preferred_element_type=F32)
    up = jnp.dot(u, wu_ref[...], preferred_element_type=F32)
    act = (gate * jax.nn.sigmoid(gate) * up).astype(BF16)
    acc_ref[...] += jnp.dot(act, wd_ref[...], preferred_element_type=F32)

    @pl.when(k == pl.num_programs(1) - 1)
    def _():
        z = ALPHA * x_ref[...] + mod_ref[5] * acc_ref[...]
        o_ref[...] = _layer_norm(z, lng_ref[1:2, :], lnb_ref[1:2, :])


def _ffn(x, mod, w_gu, w_down, ln_g, ln_b, l, tm, rows_per_mod):
    t = x.shape[0]
    tf = 512
    nf = D_FF // tf
    return pl.pallas_call(
        _ffn_kernel,
        grid=(t // tm, nf),
        in_specs=[
            pl.BlockSpec((tm, D_MODEL), lambda i, k: (i, 0)),
            _mod_spec(mod, tm, rows_per_mod),
            pl.BlockSpec((None, D_MODEL, tf), lambda i, k: (l, 0, k)),
            pl.BlockSpec((None, D_MODEL, tf), lambda i, k: (l, 0, k + nf)),
            pl.BlockSpec((None, tf, D_MODEL), lambda i, k: (l, k, 0)),
            pl.BlockSpec((None, 2, D_MODEL), lambda i, k: (l, 0, 0)),
            pl.BlockSpec((None, 2, D_MODEL), lambda i, k: (l, 0, 0)),
        ],
        out_specs=pl.BlockSpec((tm, D_MODEL), lambda i, k: (i, 0)),
        out_shape=jax.ShapeDtypeStruct((t, D_MODEL), F32),
        scratch_shapes=[pltpu.VMEM((tm, D_MODEL), BF16), pltpu.VMEM((tm, D_MODEL), F32)],
        compiler_params=_cparams("parallel", "arbitrary"),
        name="ffn",
    )(x, mod, w_gu, w_gu, w_down, ln_g, ln_b)


def _t5_bucket(dist):
    exact = N_BUCKETS // 2
    n = jnp.maximum(dist, 0)
    log_ratio = jnp.log(jnp.maximum(n, exact).astype(F32) / exact) / math.log(T5_MAX_DIST / exact)
    large = jnp.minimum(exact + (log_ratio * (N_BUCKETS - exact)).astype(jnp.int32), N_BUCKETS - 1)
    return jnp.where(n < exact, n, large)


def _bias_tables(t5_table):
    tps, tcs, lanes, news = [], [], [], []
    for g, (w, d) in enumerate(DILATIONS):
        bucket = _t5_bucket(d * jnp.arange(BLK + 1))
        onehot = (bucket[:, None] == jnp.arange(N_BUCKETS)[None, :]).astype(F32)
        bias = jnp.dot(onehot, t5_table[:, g * A_HEADS:(g + 1) * A_HEADS],
                       precision=lax.Precision.HIGHEST).T
        wv = jnp.concatenate([bias[:, ::-1], jnp.zeros((A_HEADS, BLK), F32)], axis=1)
        toep = jnp.tile(wv, (1, BLK))[:, :BLK * 2 * BLK].reshape(A_HEADS, BLK, 2 * BLK)
        tps.append(toep[:, :, :BLK])
        tcs.append(toep[:, :, BLK:])
        by_row = jnp.repeat(bias[:, BLK:0:-1], d, axis=1)
        t = jnp.arange(BLK * d)[None, :]
        lanes.append(jnp.where(t % d == 0, by_row, NEG_INF))
        news.append(bias[:, 0])
    new = jnp.zeros((A_HEADS, LANES), F32).at[:, :N_DIL].set(jnp.stack(news, axis=1))
    return jnp.stack(tps), jnp.stack(tcs), lanes, new


def _rope_tables(pos):
    half = HEAD_DIM // 2
    inv = ROPE_THETA ** (-jnp.arange(half, dtype=F32) / half)
    ang = pos.astype(F32)[:, None] * inv[None]
    cos, sin = jnp.cos(ang), jnp.sin(ang)
    cos_t = jnp.concatenate([cos, cos, cos, cos], axis=-1)
    sin_t = jnp.concatenate([-sin, sin, -sin, sin], axis=-1)
    return cos_t, sin_t


def kernel(x_prompt, x_sample, c_prompt, c_sample, cache_a0, cache_a1, cache_a2, cache_b, t5_table, w_ada, b_ada,
           w_in, sinks, w_pa, w_pb, w_o, w_gu, w_down, ln_g, ln_b):
    tp_rows = BATCH * SEQ
    mod = _adaln(jnp.concatenate([c_prompt, c_sample], axis=0), w_ada, b_ada)
    mod_p = mod[:, :BATCH].reshape(DEPTH, BATCH, 6, 1, D_MODEL).transpose(0, 2, 1, 3, 4)
    mod_s = mod[:, BATCH:].reshape(DEPTH, 1, DEC_BATCH, 6, D_MODEL).transpose(0, 3, 1, 2, 4)

    q_scale = jnp.concatenate([jnp.full((A_HD,), Q_SCALE, F32), jnp.ones((2 * A_HD,), F32)])
    w_a = (w_in[:, :, :A_COLS] * jnp.tile(q_scale, N_DIL)).astype(BF16)
    b_scale = jnp.concatenate([jnp.full((B_Q,), Q_SCALE, F32), jnp.ones((2 * B_KV,), F32)])
    w_b = (w_in[:, :, A_COLS:A_COLS + B_COLS] * b_scale).astype(BF16)
    w_g = w_in[:, :, A_COLS + B_COLS:].astype(BF16)
    w_pa_h, w_pb_h, w_o_h = w_pa.astype(BF16), w_pb.astype(BF16), w_o.astype(BF16)
    w_gu_h, w_down_h = w_gu.astype(BF16), w_down.astype(BF16)

    tps, tcs, bias_lanes, bias_new = _bias_tables(t5_table)
    cos_p, sin_p = _rope_tables(jnp.arange(SEQ))
    cos_s, sin_s = _rope_tables(jnp.full((DEC_BATCH,), PAST_LEN))
    caches_t = [c.transpose(0, 1, 3, 4, 5, 2) for c in (cache_a0, cache_a1, cache_a2, cache_b)]

    xp = x_prompt.reshape(tp_rows, D_MODEL)
    xs = x_sample.reshape(DEC_BATCH, D_MODEL)
    rows_a_p = [[] for _ in range(N_DIL)]
    rows_a_s = [[] for _ in range(N_DIL)]
    rows_b_p, rows_b_s = [], []
    for l in range(DEPTH):
        sink = sinks[l].reshape(B_HEADS)

        h_a = _proj_a(xp, mod_p[l], w_a, l, 512, SEQ)
        qb, kvb = _proj_b(xp, mod_p[l], w_b, l, cos_p, sin_p, 512, SEQ)
        gates = _gates(xp, mod_p[l], w_g, l, 512, SEQ)
        oa, kv_rows = _dil_attn(h_a, tps, tcs)
        ob, b_rows = _band_b(qb, kvb, sink)
        xp = _post_attn(oa, ob, gates, xp, mod_p[l], w_pa_h, w_pb_h, w_o_h, ln_g, ln_b, l, 256, SEQ)
        xp = _ffn(xp, mod_p[l], w_gu_h, w_down_h, ln_g, ln_b, l, 512, SEQ)
        for g in range(N_DIL):
            rows_a_p[g].append(kv_rows[g])
        rows_b_p.append(b_rows.reshape(BATCH, 2, B_KV_HEADS, HEAD_DIM, BLK))

        h_a = _proj_a(xs, mod_s[l], w_a, l, DEC_BATCH, DEC_BATCH)
        qb, kvb = _proj_b(xs, mod_s[l], w_b, l, cos_s, sin_s, DEC_BATCH, DEC_BATCH)
        gates = _gates(xs, mod_s[l], w_g, l, DEC_BATCH, DEC_BATCH)
        oa, ob = _sample_attn(h_a, qb, kvb, caches_t, l, bias_lanes, bias_new, sink)
        xs = _post_attn(oa, ob, gates, xs, mod_s[l], w_pa_h, w_pb_h, w_o_h, ln_g, ln_b, l, DEC_BATCH, DEC_BATCH)
        xs = _ffn(xs, mod_s[l], w_gu_h, w_down_h, ln_g, ln_b, l, DEC_BATCH, DEC_BATCH)
        ha = h_a.reshape(DEC_BATCH, 1, N_DIL, 3, A_HEADS, HEAD_DIM)
        for g in range(N_DIL):
            rows_a_s[g].append(ha[:, :, g, 1:])
        rows_b_s.append(kvb.reshape(DEC_BATCH, 1, 2, B_KV_HEADS, HEAD_DIM))

    to_rows_major = lambda parts: jnp.stack(parts).transpose(0, 1, 5, 2, 3, 4)
    outs = [xp.reshape(BATCH, SEQ, D_MODEL), xs.reshape(DEC_BATCH, 1, D_MODEL)]
    for g in range(N_DIL):
        outs.append(to_rows_major(rows_a_p[g]))
        outs.append(jnp.stack(rows_a_s[g]))
    outs.append(to_rows_major(rows_b_p))
    outs.append(jnp.stack(rows_b_s))
    return tuple(outs)
```

```python
import math

import jax
import jax.numpy as jnp
from jax import lax
from jax.experimental import pallas as pl
from jax.experimental.pallas import tpu as pltpu

F32 = jnp.float32
BF16 = jnp.bfloat16

D_MODEL = 2048
BATCH = 8
SEQ = 2048
DEPTH = 2
DEC_BATCH = 128
PAST_LEN = 8192
HEAD_DIM = 64
DILATIONS = ((128, 1), (512, 4), (2048, 16))
N_DIL = 3
A_HEADS = 8
B_HEADS = 16
B_KV_HEADS = 2
B_GROUP = B_HEADS // B_KV_HEADS
B_WINDOW = 128
ROPE_THETA = 150000.0
N_BUCKETS = 32
T5_MAX_DIST = 2048
D_FF = ((8 * D_MODEL + 3 * 256 - 1) // (3 * 256)) * 256
ALPHA = (2 * DEPTH) ** 0.25
LN_EPS = 1e-5
NEG_INF = -1e30

BLK = 128
A_HD = A_HEADS * HEAD_DIM
A_GROUP_COLS = 3 * A_HD
A_COLS = N_DIL * A_GROUP_COLS
B_Q = B_HEADS * HEAD_DIM
B_KV = B_KV_HEADS * HEAD_DIM
B_COLS = B_Q + 2 * B_KV
G_COLS = 2 * D_MODEL
LANES = 128
PAIRS_PER_STEP = 2
UNIT_UNROLL = 4
Q_SCALE = HEAD_DIM ** -0.5
A_KEEP = tuple(min(w, SEQ) for w, _ in DILATIONS)

_VMEM_LIMIT = 56 * 1024 * 1024


def _cparams(*sem):
    return pltpu.CompilerParams(dimension_semantics=sem, vmem_limit_bytes=_VMEM_LIMIT)


def _mod_spec(mod, tm, rows_per_mod):
    mr = mod.shape[2]
    return pl.BlockSpec((6, None, mr, D_MODEL), lambda i, *_: (0, (i * tm) // rows_per_mod, 0, 0))


def _layer_norm(z, g, b):
    mu = jnp.mean(z, axis=-1, keepdims=True)
    zc = z - mu
    var = jnp.mean(zc * zc, axis=-1, keepdims=True)
    return zc * lax.rsqrt(var + LN_EPS) * g + b


def _adaln_kernel(c_ref, w_ref, b_ref, o_ref):
    c = c_ref[...]
    s = (c * jax.nn.sigmoid(c)).astype(BF16)
    o_ref[...] = jnp.dot(s, w_ref[...].astype(BF16), preferred_element_type=F32) + b_ref[...]


def _adaln(c_all, w_ada, b_ada):
    rows = c_all.shape[0]
    tn = 1024
    return pl.pallas_call(
        _adaln_kernel,
        grid=(DEPTH, 6 * D_MODEL // tn),
        in_specs=[
            pl.BlockSpec((rows, D_MODEL), lambda l, j: (0, 0)),
            pl.BlockSpec((None, D_MODEL, tn), lambda l, j: (l, 0, j)),
            pl.BlockSpec((None, 1, tn), lambda l, j: (l, 0, j)),
        ],
        out_specs=pl.BlockSpec((None, rows, tn), lambda l, j: (l, 0, j)),
        out_shape=jax.ShapeDtypeStruct((DEPTH, rows, 6 * D_MODEL), F32),
        compiler_params=_cparams("parallel", "parallel"),
        name="adaln",
    )(c_all, w_ada, b_ada.reshape(DEPTH, 1, 6 * D_MODEL))


def _modulate(x_ref, mod_ref, shift_idx, scale_idx):
    return (x_ref[...] * (1.0 + mod_ref[scale_idx]) + mod_ref[shift_idx]).astype(BF16)


def _proj_a_kernel(x_ref, mod_ref, w_ref, o_ref, u_ref):
    @pl.when(pl.program_id(1) == 0)
    def _():
        u_ref[...] = _modulate(x_ref, mod_ref, 0, 1)

    o_ref[...] = jnp.dot(u_ref[...], w_ref[...], preferred_element_type=F32)


def _proj_a(x, mod, w_a, l, tm, rows_per_mod):
    t = x.shape[0]
    return pl.pallas_call(
        _proj_a_kernel,
        grid=(t // tm, N_DIL),
        in_specs=[
            pl.BlockSpec((tm, D_MODEL), lambda i, g: (i, 0)),
            _mod_spec(mod, tm, rows_per_mod),
            pl.BlockSpec((None, D_MODEL, A_GROUP_COLS), lambda i, g: (l, 0, g)),
        ],
        out_specs=pl.BlockSpec((tm, A_GROUP_COLS), lambda i, g: (i, g)),
        out_shape=jax.ShapeDtypeStruct((t, A_COLS), F32),
        scratch_shapes=[pltpu.VMEM((tm, D_MODEL), BF16)],
        compiler_params=_cparams("parallel", "arbitrary"),
        name="proj_a",
    )(x, mod, w_a)


def _proj_b_kernel(x_ref, mod_ref, w_ref, cos_ref, sin_ref, q_ref, kv_ref):
    u = _modulate(x_ref, mod_ref, 0, 1)
    acc = jnp.dot(u, w_ref[...], preferred_element_type=F32)
    cos = cos_ref[...]
    sin = sin_ref[...]
    lane = lax.broadcasted_iota(jnp.int32, cos.shape, 1)
    first_half = (lane % HEAD_DIM) < (HEAD_DIM // 2)

    def rope(v):
        partner = jnp.where(first_half,
                            pltpu.roll(v, LANES - HEAD_DIM // 2, axis=1),
                            pltpu.roll(v, HEAD_DIM // 2, axis=1))
        return v * cos + partner * sin

    for c in range(B_Q // LANES):
        q_ref[:, c * LANES:(c + 1) * LANES] = rope(acc[:, c * LANES:(c + 1) * LANES]).astype(BF16)
    kv_ref[:, :B_KV] = rope(acc[:, B_Q:B_Q + B_KV])
    kv_ref[:, B_KV:] = acc[:, B_Q + B_KV:]


def _proj_b(x, mod, w_b, l, cos_t, sin_t, tm, rows_per_mod):
    t = x.shape[0]
    n_pos_blocks = cos_t.shape[0] // tm
    return pl.pallas_call(
        _proj_b_kernel,
        grid=(t // tm,),
        in_specs=[
            pl.BlockSpec((tm, D_MODEL), lambda i: (i, 0)),
            _mod_spec(mod, tm, rows_per_mod),
            pl.BlockSpec((None, D_MODEL, B_COLS), lambda i: (l, 0, 0)),
            pl.BlockSpec((tm, LANES), lambda i: (i % n_pos_blocks, 0)),
            pl.BlockSpec((tm, LANES), lambda i: (i % n_pos_blocks, 0)),
        ],
        out_specs=[
            pl.BlockSpec((tm, B_Q), lambda i: (i, 0)),
            pl.BlockSpec((tm, 2 * B_KV), lambda i: (i, 0)),
        ],
        out_shape=[
            jax.ShapeDtypeStruct((t, B_Q), BF16),
            jax.ShapeDtypeStruct((t, 2 * B_KV), F32),
        ],
        compiler_params=_cparams("parallel"),
        name="proj_b",
    )(x, mod, w_b, cos_t, sin_t)


def _gates_kernel(x_ref, mod_ref, w_ref, o_ref, u_ref):
    @pl.when(pl.program_id(1) == 0)
    def _():
        u_ref[...] = _modulate(x_ref, mod_ref, 0, 1)

    acc = jnp.dot(u_ref[...], w_ref[...], preferred_element_type=F32)
    o_ref[...] = jax.nn.sigmoid(acc).astype(o_ref.dtype)


def _gates(x, mod, w_g, l, tm, rows_per_mod):
    t = x.shape[0]
    tn = 1024
    return pl.pallas_call(
        _gates_kernel,
        grid=(t // tm, G_COLS // tn),
        in_specs=[
            pl.BlockSpec((tm, D_MODEL), lambda i, j: (i, 0)),
            _mod_spec(mod, tm, rows_per_mod),
            pl.BlockSpec((None, D_MODEL, tn), lambda i, j: (l, 0, j)),
        ],
        out_specs=pl.BlockSpec((tm, tn), lambda i, j: (i, j)),
        out_shape=jax.ShapeDtypeStruct((t, G_COLS), BF16),
        scratch_shapes=[pltpu.VMEM((tm, D_MODEL), BF16)],
        compiler_params=_cparams("parallel", "arbitrary"),
        name="gates",
    )(x, mod, w_g)


def _dot_nt(a, b):
    return lax.dot_general(a, b, (((1,), (1,)), ((), ())), preferred_element_type=F32)


def _band_masks():
    row = lax.broadcasted_iota(jnp.int32, (2 * BLK, BLK), 0) % BLK
    col = lax.broadcasted_iota(jnp.int32, (2 * BLK, BLK), 1)
    return col >= row, col <= row


def _rows(start, d):
    return pl.ds(start, BLK, stride=d) if d > 1 else pl.ds(pl.multiple_of(start, BLK), BLK)


def _stack_pair(q):
    lo = lax.broadcasted_iota(jnp.int32, (BLK, LANES), 1) < HEAD_DIM
    zero = jnp.zeros_like(q)
    return jnp.concatenate([jnp.where(lo, q, zero), jnp.where(lo, zero, q)], axis=0)


def _unstack_pair(x):
    lo = lax.broadcasted_iota(jnp.int32, (BLK, LANES), 1) < HEAD_DIM
    return jnp.where(lo, jnp.broadcast_to(x[:BLK], (BLK, LANES)), jnp.broadcast_to(x[BLK:], (BLK, LANES)))


def _pair_attend(qs, cur, prev, has_prev, sink):
    valid_p, valid_c = _band_masks()

    def scores(k, bias, valid):
        s = _dot_nt(qs, k)
        if bias is not None:
            s = s + bias
        return jnp.where(valid, s, NEG_INF)

    sc = scores(cur[0], cur[2], valid_c)
    if prev is not None:
        sp = jnp.where(has_prev, scores(prev[0], prev[2], valid_p), NEG_INF)
        m = jnp.max(jnp.maximum(sp, sc), axis=-1, keepdims=True)
    else:
        m = jnp.max(sc, axis=-1, keepdims=True)
    if sink is not None:
        m = jnp.maximum(m, sink)
    pc = jnp.exp(sc - m)
    pv = jnp.dot(pc.astype(BF16), cur[1], preferred_element_type=F32)
    if prev is not None:
        pp = jnp.exp(sp - m)
        pv = pv + jnp.dot(pp.astype(BF16), prev[1], preferred_element_type=F32)
        den = jnp.sum(pp + pc, axis=-1, keepdims=True)
    else:
        den = jnp.sum(pc, axis=-1, keepdims=True)
    if sink is not None:
        den = den + jnp.exp(sink - m)
    return _unstack_pair(pv), _unstack_pair(m), _unstack_pair(den)


def _dil_group(gi, q_refs, k_refs, v_refs, tp_ref, tc_ref, acc_ref, m_ref, l_ref):
    d = DILATIONS[gi][1]
    nblk = SEQ // (BLK * d)

    def unit(u, carry):
        r = u // nblk
        j = u % nblk
        start = j * (BLK * d) + r
        start_prev = jnp.maximum(j - 1, 0) * (BLK * d) + r
        rows = _rows(start, d)
        rows_prev = _rows(start_prev, d)
        for p in range(PAIRS_PER_STEP):
            qs = _stack_pair(q_refs[p][rows, :].astype(BF16))
            cur = (k_refs[p][rows, :].astype(BF16), v_refs[p][rows, :].astype(BF16),
                   tc_ref[2 * p:2 * p + 2].reshape(2 * BLK, BLK))
            prev = None
            if nblk > 1:
                prev = (k_refs[p][rows_prev, :].astype(BF16), v_refs[p][rows_prev, :].astype(BF16),
                        tp_ref[2 * p:2 * p + 2].reshape(2 * BLK, BLK))
            num, m_g, l_g = _pair_attend(qs, cur, prev, j > 0, None)
            if _group_step(gi) == 0:
                acc_ref[p, rows, :] = num
                m_ref[p, rows, :] = m_g
                l_ref[p, rows, :] = l_g
            else:
                m_old = m_ref[p, rows, :]
                m_new = jnp.maximum(m_old, m_g)
                a = jnp.exp(m_old - m_new)
                b = jnp.exp(m_g - m_new)
                acc_ref[p, rows, :] = acc_ref[p, rows, :] * a + num * b
                l_ref[p, rows, :] = l_ref[p, rows, :] * a + l_g * b
                m_ref[p, rows, :] = m_new
        return carry

    lax.fori_loop(0, SEQ // BLK, unit, 0, unroll=UNIT_UNROLL)


def _group_step(gi):
    return N_DIL - 1 - gi


def _dil_attn_kernel(q0, q1, k0, k1, v0, v1, tp_ref, tc_ref, o_ref, c0_ref, c1_ref, c2_ref, acc_ref, m_ref, l_ref):
    g = pl.program_id(2)
    q_refs, k_refs, v_refs = (q0, q1), (k0, k1), (v0, v1)
    c_refs = (c0_ref, c1_ref, c2_ref)
    for gi in range(N_DIL):
        @pl.when(g == _group_step(gi))
        def _(gi=gi):
            _dil_group(gi, q_refs, k_refs, v_refs, tp_ref, tc_ref, acc_ref, m_ref, l_ref)
            keep = A_KEEP[gi]
            for p in range(PAIRS_PER_STEP):
                c_refs[gi][0, 2 * p:2 * p + 2] = k_refs[p][SEQ - keep:, :].T.reshape(2, HEAD_DIM, keep)
                c_refs[gi][1, 2 * p:2 * p + 2] = v_refs[p][SEQ - keep:, :].T.reshape(2, HEAD_DIM, keep)

    @pl.when(g == N_DIL - 1)
    def _():
        for p in range(PAIRS_PER_STEP):
            o_ref[:, p * LANES:(p + 1) * LANES] = (acc_ref[p] / l_ref[p]).astype(BF16)


def _dil_attn(h_a, tps, tcs):
    hv = h_a.reshape(BATCH, SEQ, A_COLS)
    steps = A_HEADS // 2 // PAIRS_PER_STEP
    heads = 2 * PAIRS_PER_STEP
    slab = (None, SEQ, LANES)
    col = lambda part, p: (lambda b, s, g: (b, 0, _group_step(g) * 12 + part * 4 + s * PAIRS_PER_STEP + p))
    tbl = pl.BlockSpec((None, heads, BLK, BLK), lambda b, s, g: (_group_step(g), s, 0, 0))
    cache = lambda keep: pl.BlockSpec((None, 2, heads, HEAD_DIM, keep), lambda b, s, g: (b, 0, s, 0, 0))
    outs = pl.pallas_call(
        _dil_attn_kernel,
        grid=(BATCH, steps, N_DIL),
        in_specs=[pl.BlockSpec(slab, col(part, p)) for part in range(3) for p in range(PAIRS_PER_STEP)] + [tbl, tbl],
        out_specs=[pl.BlockSpec((None, SEQ, heads * HEAD_DIM), lambda b, s, g: (b, 0, s))]
        + [cache(keep) for keep in A_KEEP],
        out_shape=[jax.ShapeDtypeStruct((BATCH, SEQ, A_HD), BF16)]
        + [jax.ShapeDtypeStruct((BATCH, 2, A_HEADS, HEAD_DIM, keep), F32) for keep in A_KEEP],
        scratch_shapes=[pltpu.VMEM((PAIRS_PER_STEP, SEQ, LANES), F32)] * 3,
        compiler_params=_cparams("parallel", "parallel", "arbitrary"),
        name="dil_attn",
    )(*([hv] * (3 * PAIRS_PER_STEP)), tps, tcs)
    return outs[0].reshape(BATCH * SEQ, A_HD), outs[1:]


def _band_b_kernel(sink_ref, q_ref, kvp_ref, kvc_ref, o_ref, c_ref):
    j = pl.program_id(1)
    lo = lax.broadcasted_iota(jnp.int32, (BLK, LANES), 1) < HEAD_DIM
    first = lax.broadcasted_iota(jnp.int32, (2 * BLK, 1), 0) < BLK

    def per_kv_head(x):
        rolled = pltpu.roll(x, HEAD_DIM, axis=1)
        return jnp.where(lo, x, rolled).astype(BF16), jnp.where(lo, rolled, x).astype(BF16)

    kp, kc = per_kv_head(kvp_ref[:, :B_KV]), per_kv_head(kvc_ref[:, :B_KV])
    vp, vc = per_kv_head(kvp_ref[:, B_KV:]), per_kv_head(kvc_ref[:, B_KV:])
    for pair in range(B_HEADS // 2):
        hk = (2 * pair) // B_GROUP
        sl = slice(pair * LANES, (pair + 1) * LANES)
        sink = jnp.where(first, sink_ref[2 * pair], sink_ref[2 * pair + 1])
        num, _, den = _pair_attend(_stack_pair(q_ref[:, sl]), (kc[hk], vc[hk], None), (kp[hk], vp[hk], None),
                                   j > 0, sink)
        o_ref[:, sl] = (num / den).astype(BF16)

    @pl.when(j == pl.num_programs(1) - 1)
    def _():
        c_ref[...] = kvc_ref[...].T


def _band_b(qb, kvb, sink):
    nb = SEQ // BLK
    qv = qb.reshape(BATCH, SEQ, B_Q)
    kvv = kvb.reshape(BATCH, SEQ, 2 * B_KV)
    prev = lambda j: jnp.maximum(j - 1, 0)
    o, c = pl.pallas_call(
        _band_b_kernel,
        grid=(BATCH, nb),
        in_specs=[
            pl.BlockSpec(memory_space=pltpu.SMEM),
            pl.BlockSpec((None, BLK, B_Q), lambda b, j: (b, j, 0)),
            pl.BlockSpec((None, BLK, 2 * B_KV), lambda b, j: (b, prev(j), 0)),
            pl.BlockSpec((None, BLK, 2 * B_KV), lambda b, j: (b, j, 0)),
        ],
        out_specs=[
            pl.BlockSpec((None, BLK, B_Q), lambda b, j: (b, j, 0)),
            pl.BlockSpec((None, 2 * B_KV, BLK), lambda b, j: (b, 0, 0)),
        ],
        out_shape=[
            jax.ShapeDtypeStruct((BATCH, SEQ, B_Q), BF16),
            jax.ShapeDtypeStruct((BATCH, 2 * B_KV, BLK), F32),
        ],
        compiler_params=_cparams("parallel", "arbitrary"),
        name="band_b",
    )(sink, qv, kvv, kvv)
    return o.reshape(BATCH * SEQ, B_Q), c


_COL_QA, _COL_KA, _COL_VA = 0, N_DIL * A_HEADS, 2 * N_DIL * A_HEADS
_COL_QB = 3 * N_DIL * A_HEADS
_COL_KB = _COL_QB + B_HEADS
_COL_VB = _COL_KB + B_KV_HEADS
_N_COLS = _COL_VB + B_KV_HEADS

_LANE_CHUNK = 512


def _sample_attn_kernel(cols_ref, c0_ref, c1_ref, c2_ref, cb_ref, b0_ref, b1_ref, b2_ref, bnew_ref, sink_ref, o_ref):
    caches = (c0_ref, c1_ref, c2_ref)
    biases = (b0_ref, b1_ref, b2_ref)

    def cols(start, count):
        return jnp.stack([cols_ref[:, start + i:start + i + 1] for i in range(count)])

    def attend(q, k_new, v_new, kt_ref, vt_ref, bias_ref, bias_new, sink):
        n = kt_ref.shape[-1]
        chunk = min(n, _LANE_CHUNK)
        parts = []
        for c in range(n // chunk):
            sl = slice(c * chunk, (c + 1) * chunk)
            s = jnp.sum(kt_ref[:, :, sl] * q, axis=1, keepdims=True)
            if bias_ref is not None:
                bias = bias_ref[:, :, sl]
                s = jnp.where(bias > 0.5 * NEG_INF, s + bias, NEG_INF)
            parts.append(s)
        s_new = jnp.sum(q * k_new, axis=1, keepdims=True)
        if bias_new is not None:
            s_new = s_new + bias_new
        m = s_new
        for s in parts:
            m = jnp.maximum(m, jnp.max(s, axis=-1, keepdims=True))
        if sink is not None:
            m = jnp.maximum(m, sink)
        p_new = jnp.exp(s_new - m)
        den = p_new
        pv = p_new * v_new
        for c, s in enumerate(parts):
            p = jnp.exp(s - m)
            den = den + jnp.sum(p, axis=-1, keepdims=True)
            pv = pv + jnp.sum(vt_ref[:, :, c * chunk:(c + 1) * chunk] * p, axis=-1, keepdims=True)
        if sink is not None:
            den = den + jnp.exp(sink - m)
        return pv / den, m + jnp.log(den)

    os_, ls_ = [], []
    for g in range(N_DIL):
        c = g * A_HEADS
        o, lse = attend(cols(_COL_QA + c, A_HEADS), cols(_COL_KA + c, A_HEADS), cols(_COL_VA + c, A_HEADS),
                        caches[g].at[0], caches[g].at[1], biases[g], bnew_ref[:, :, g:g + 1], None)
        os_.append(o)
        ls_.append(lse)
    mx = jnp.maximum(jnp.maximum(ls_[0], ls_[1]), ls_[2])
    es = [jnp.exp(l - mx) for l in ls_]
    out_a = (es[0] * os_[0] + es[1] * os_[1] + es[2] * os_[2]) / (es[0] + es[1] + es[2])
    out_b = []
    for hk in range(B_KV_HEADS):
        o, _ = attend(cols(_COL_QB + hk * B_GROUP, B_GROUP), cols(_COL_KB + hk, 1), cols(_COL_VB + hk, 1),
                      cb_ref.at[0, hk:hk + 1], cb_ref.at[1, hk:hk + 1], None, None,
                      sink_ref[hk * B_GROUP:(hk + 1) * B_GROUP])
        out_b.append(o)

    lane = lax.broadcasted_iota(jnp.int32, (HEAD_DIM, LANES), 1)
    out = jnp.zeros((HEAD_DIM, LANES), F32)
    for h in range(A_HEADS):
        out = jnp.where(lane == h, out_a[h], out)
    for hq in range(B_HEADS):
        out = jnp.where(lane == A_HEADS + hq, out_b[hq // B_GROUP][hq % B_GROUP], out)
    o_ref[...] = out


def _sample_attn(h_a, qb, kvb, caches_t, l, bias_lanes, bias_new, sink):
    n = DEC_BATCH
    ha = h_a.reshape(n, N_DIL, 3, A_HEADS, HEAD_DIM)
    kb = kvb.reshape(n, 2, B_KV_HEADS, HEAD_DIM)
    cols = jnp.concatenate([
        ha[:, :, 0].reshape(n, -1, HEAD_DIM), ha[:, :, 1].reshape(n, -1, HEAD_DIM), ha[:, :, 2].reshape(n, -1, HEAD_DIM),
        qb.astype(F32).reshape(n, B_HEADS, HEAD_DIM), kb[:, 0], kb[:, 1]], axis=1)
    cols = cols.transpose(0, 2, 1)
    cache_spec = lambda c: pl.BlockSpec((None, None) + c.shape[2:], lambda b: (l, b, 0, 0, 0, 0))
    full = lambda a: pl.BlockSpec(a.shape, lambda b: (0,) * a.ndim)
    small = [a[:, None, :] for a in bias_lanes] + [bias_new[:, None, :], sink.reshape(B_HEADS, 1, 1)]
    out = pl.pallas_call(
        _sample_attn_kernel,
        grid=(n,),
        in_specs=[pl.BlockSpec((None, HEAD_DIM, _N_COLS), lambda b: (b, 0, 0))]
        + [cache_spec(c) for c in caches_t] + [full(a) for a in small],
        out_specs=pl.BlockSpec((None, HEAD_DIM, LANES), lambda b: (b, 0, 0)),
        out_shape=jax.ShapeDtypeStruct((n, HEAD_DIM, LANES), F32),
        compiler_params=_cparams("parallel"),
        name="sample_attn",
    )(cols, *caches_t, *small)
    heads = out.transpose(0, 2, 1)
    out_a = heads[:, :A_HEADS].reshape(n, A_HD)
    out_b = heads[:, A_HEADS:A_HEADS + B_HEADS].reshape(n, B_Q)
    return out_a.astype(BF16), out_b.astype(BF16)


def _post_attn_kernel(a_ref, ob_ref, gate_ref, x_ref, mod_ref, wpa_ref, wpb_ref, wo_ref, lng_ref, lnb_ref, o_ref):
    ya = jnp.dot(a_ref[...], wpa_ref[...], preferred_element_type=F32)
    yb = jnp.dot(ob_ref[...], wpb_ref[...], preferred_element_type=F32)
    mix = gate_ref[:, :D_MODEL].astype(F32) * ya + gate_ref[:, D_MODEL:].astype(F32) * yb
    y = jnp.dot(mix.astype(BF16), wo_ref[...], preferred_element_type=F32)
    z = ALPHA * x_ref[...] + mod_ref[2] * y
    o_ref[...] = _layer_norm(z, lng_ref[0:1, :], lnb_ref[0:1, :])


def _post_attn(oa, ob, gates, x, mod, w_pa, w_pb, w_o, ln_g, ln_b, l, tm, rows_per_mod):
    t = x.shape[0]
    row = lambda w: pl.BlockSpec((tm, w), lambda i: (i, 0))
    const = lambda shape: pl.BlockSpec((None,) + shape, lambda i: (l, 0, 0), pipeline_mode=pl.Buffered(1))
    return pl.pallas_call(
        _post_attn_kernel,
        grid=(t // tm,),
        in_specs=[
            row(A_HD), row(B_Q), row(G_COLS), row(D_MODEL),
            _mod_spec(mod, tm, rows_per_mod),
            const((A_HD, D_MODEL)), const((B_Q, D_MODEL)), const((D_MODEL, D_MODEL)),
            const((2, D_MODEL)), const((2, D_MODEL)),
        ],
        out_specs=row(D_MODEL),
        out_shape=jax.ShapeDtypeStruct((t, D_MODEL), F32),
        compiler_params=_cparams("parallel"),
        name="post_attn",
    )(oa, ob, gates, x, mod, w_pa, w_pb, w_o, ln_g, ln_b)


def _ffn_kernel(x_ref, mod_ref, wg_ref, wu_ref, wd_ref, lng_ref, lnb_ref, o_ref, u_ref, acc_ref):
    k = pl.program_id(1)

    @pl.when(k == 0)
    def _():
        u_ref[...] = _modulate(x_ref, mod_ref, 3, 4)
        acc_ref[...] = jnp.zeros_like(acc_ref)

    u = u_ref[...]
    gate = jnp.dot(u, wg_ref[...], preferred_element_type=F32)
    up = jnp.dot(u, wu_ref[...], preferred_element_type=F32)
    act = (gate * jax.nn.sigmoid(gate) * up).astype(BF16)
    acc_ref[...] += jnp.dot(act, wd_ref[...], preferred_element_type=F32)

    @pl.when(k == pl.num_programs(1) - 1)
    def _():
        z = ALPHA * x_ref[...] + mod_ref[5] * acc_ref[...]
        o_ref[...] = _layer_norm(z, lng_ref[1:2, :], lnb_ref[1:2, :])


def _ffn(x, mod, w_gu, w_down, ln_g, ln_b, l, tm, rows_per_mod):
    t = x.shape[0]
    tf = 512
    nf = D_FF // tf
    return pl.pallas_call(
        _ffn_kernel,
        grid=(t // tm, nf),
        in_specs=[
            pl.BlockSpec((tm, D_MODEL), lambda i, k: (i, 0)),
            _mod_spec(mod, tm, rows_per_mod),
            pl.BlockSpec((None, D_MODEL, tf), lambda i, k: (l, 0, k)),
            pl.BlockSpec((None, D_MODEL, tf), lambda i, k: (l, 0, k + nf)),
            pl.BlockSpec((None, tf, D_MODEL), lambda i, k: (l, k, 0)),
            pl.BlockSpec((None, 2, D_MODEL), lambda i, k: (l, 0, 0)),
            pl.BlockSpec((None, 2, D_MODEL), lambda i, k: (l, 0, 0)),
        ],
        out_specs=pl.BlockSpec((tm, D_MODEL), lambda i, k: (i, 0)),
        out_shape=jax.ShapeDtypeStruct((t, D_MODEL), F32),
        scratch_shapes=[pltpu.VMEM((tm, D_MODEL), BF16), pltpu.VMEM((tm, D_MODEL), F32)],
        compiler_params=_cparams("parallel", "arbitrary"),
        name="ffn",
    )(x, mod, w_gu, w_gu, w_down, ln_g, ln_b)


def _t5_bucket(dist):
    exact = N_BUCKETS // 2
    n = jnp.maximum(dist, 0)
    log_ratio = jnp.log(jnp.maximum(n, exact).astype(F32) / exact) / math.log(T5_MAX_DIST / exact)
    large = jnp.minimum(exact + (log_ratio * (N_BUCKETS - exact)).astype(jnp.int32), N_BUCKETS - 1)
    return jnp.where(n < exact, n, large)


def _bias_tables(t5_table):
    tps, tcs, lanes, news = [], [], [], []
    for g, (w, d) in enumerate(DILATIONS):
        bucket = _t5_bucket(d * jnp.arange(BLK + 1))
        onehot = (bucket[:, None] == jnp.arange(N_BUCKETS)[None, :]).astype(F32)
        bias = jnp.dot(onehot, t5_table[:, g * A_HEADS:(g + 1) * A_HEADS],
                       precision=lax.Precision.HIGHEST).T
        wv = jnp.concatenate([bias[:, ::-1], jnp.zeros((A_HEADS, BLK), F32)], axis=1)
        toep = jnp.tile(wv, (1, BLK))[:, :BLK * 2 * BLK].reshape(A_HEADS, BLK, 2 * BLK)
        tps.append(toep[:, :, :BLK])
        tcs.append(toep[:, :, BLK:])
        by_row = jnp.repeat(bias[:, BLK:0:-1], d, axis=1)
        t = jnp.arange(BLK * d)[None, :]
        lanes.append(jnp.where(t % d == 0, by_row, NEG_INF))
        news.append(bias[:, 0])
    new = jnp.zeros((A_HEADS, LANES), F32).at[:, :N_DIL].set(jnp.stack(news, axis=1))
    return jnp.stack(tps), jnp.stack(tcs), lanes, new


def _rope_tables(pos):
    half = HEAD_DIM // 2
    inv = ROPE_THETA ** (-jnp.arange(half, dtype=F32) / half)
    ang = pos.astype(F32)[:, None] * inv[None]
    cos, sin = jnp.cos(ang), jnp.sin(ang)
    cos_t = jnp.concatenate([cos, cos, cos, cos], axis=-1)
    sin_t = jnp.concatenate([-sin, sin, -sin, sin], axis=-1)
    return cos_t, sin_t


def kernel(x_prompt, x_sample, c_prompt, c_sample, cache_a0, cache_a1, cache_a2, cache_b, t5_table, w_ada, b_ada,
           w_in, sinks, w_pa, w_pb, w_o, w_gu, w_down, ln_g, ln_b):
    tp_rows = BATCH * SEQ
    mod = _adaln(jnp.concatenate([c_prompt, c_sample], axis=0), w_ada, b_ada)
    mod_p = mod[:, :BATCH].reshape(DEPTH, BATCH, 6, 1, D_MODEL).transpose(0, 2, 1, 3, 4)
    mod_s = mod[:, BATCH:].reshape(DEPTH, 1, DEC_BATCH, 6, D_MODEL).transpose(0, 3, 1, 2, 4)

    q_scale = jnp.concatenate([jnp.full((A_HD,), Q_SCALE, F32), jnp.ones((2 * A_HD,), F32)])
    w_a = (w_in[:, :, :A_COLS] * jnp.tile(q_scale, N_DIL)).astype(BF16)
    b_scale = jnp.concatenate([jnp.full((B_Q,), Q_SCALE, F32), jnp.ones((2 * B_KV,), F32)])
    w_b = (w_in[:, :, A_COLS:A_COLS + B_COLS] * b_scale).astype(BF16)
    w_g = w_in[:, :, A_COLS + B_COLS:].astype(BF16)
    w_pa_h, w_pb_h, w_o_h = w_pa.astype(BF16), w_pb.astype(BF16), w_o.astype(BF16)
    w_gu_h, w_down_h = w_gu.astype(BF16), w_down.astype(BF16)

    tps, tcs, bias_lanes, bias_new = _bias_tables(t5_table)
    cos_p, sin_p = _rope_tables(jnp.arange(SEQ))
    cos_s, sin_s = _rope_tables(jnp.full((DEC_BATCH,), PAST_LEN))
    caches_t = [c.transpose(0, 1, 3, 4, 5, 2) for c in (cache_a0, cache_a1, cache_a2, cache_b)]

    xp = x_prompt.reshape(tp_rows, D_MODEL)
    xs = x_sample.reshape(DEC_BATCH, D_MODEL)
    rows_a_p = [[] for _ in range(N_DIL)]
    rows_a_s = [[] for _ in range(N_DIL)]
    rows_b_p, rows_b_s = [], []
    for l in range(DEPTH):
        sink = sinks[l].reshape(B_HEADS)

        h_a = _proj_a(xp, mod_p[l], w_a, l, 1024, SEQ)
        qb, kvb = _proj_b(xp, mod_p[l], w_b, l, cos_p, sin_p, 512, SEQ)
        gates = _gates(xp, mod_p[l], w_g, l, 1024, SEQ)
        oa, kv_rows = _dil_attn(h_a, tps, tcs)
        ob, b_rows = _band_b(qb, kvb, sink)
        xp = _post_attn(oa, ob, gates, xp, mod_p[l], w_pa_h, w_pb_h, w_o_h, ln_g, ln_b, l, 256, SEQ)
        xp = _ffn(xp, mod_p[l], w_gu_h, w_down_h, ln_g, ln_b, l, 512, SEQ)
        for g in range(N_DIL):
            rows_a_p[g].append(kv_rows[g])
        rows_b_p.append(b_rows.reshape(BATCH, 2, B_KV_HEADS, HEAD_DIM, BLK))

        h_a = _proj_a(xs, mod_s[l], w_a, l, DEC_BATCH, DEC_BATCH)
        qb, kvb = _proj_b(xs, mod_s[l], w_b, l, cos_s, sin_s, DEC_BATCH, DEC_BATCH)
        gates = _gates(xs, mod_s[l], w_g, l, DEC_BATCH, DEC_BATCH)
        oa, ob = _sample_attn(h_a, qb, kvb, caches_t, l, bias_lanes, bias_new, sink)
        xs = _post_attn(oa, ob, gates, xs, mod_s[l], w_pa_h, w_pb_h, w_o_h, ln_g, ln_b, l, DEC_BATCH, DEC_BATCH)
        xs = _ffn(xs, mod_s[l], w_gu_h, w_down_h, ln_g, ln_b, l, DEC_BATCH, DEC_BATCH)
        ha = h_a.reshape(DEC_BATCH, 1, N_DIL, 3, A_HEADS, HEAD_DIM)
        for g in range(N_DIL):
            rows_a_s[g].append(ha[:, :, g, 1:])
        rows_b_s.append(kvb.reshape(DEC_BATCH, 1, 2, B_KV_HEADS, HEAD_DIM))

    to_rows_major = lambda parts: jnp.stack(parts).transpose(0, 1, 5, 2, 3, 4)
    outs = [xp.reshape(BATCH, SEQ, D_MODEL), xs.reshape(DEC_BATCH, 1, D_MODEL)]
    for g in range(N_DIL):
        outs.append(to_rows_major(rows_a_p[g]))
        outs.append(jnp.stack(rows_a_s[g]))
    outs.append(to_rows_major(rows_b_p))
    outs.append(jnp.stack(rows_b_s))
    return tuple(outs)
```

```python
import math

import jax
import jax.numpy as jnp
from jax import lax
from jax.experimental import pallas as pl
from jax.experimental.pallas import tpu as pltpu

F32 = jnp.float32
BF16 = jnp.bfloat16

D_MODEL = 2048
BATCH = 8
SEQ = 2048
DEPTH = 2
DEC_BATCH = 128
PAST_LEN = 8192
HEAD_DIM = 64
DILATIONS = ((128, 1), (512, 4), (2048, 16))
N_DIL = 3
A_HEADS = 8
B_HEADS = 16
B_KV_HEADS = 2
B_GROUP = B_HEADS // B_KV_HEADS
B_WINDOW = 128
ROPE_THETA = 150000.0
N_BUCKETS = 32
T5_MAX_DIST = 2048
D_FF = ((8 * D_MODEL + 3 * 256 - 1) // (3 * 256)) * 256
ALPHA = (2 * DEPTH) ** 0.25
LN_EPS = 1e-5
NEG_INF = -1e30

BLK = 128
A_HD = A_HEADS * HEAD_DIM
A_GROUP_COLS = 3 * A_HD
A_COLS = N_DIL * A_GROUP_COLS
B_Q = B_HEADS * HEAD_DIM
B_KV = B_KV_HEADS * HEAD_DIM
B_COLS = B_Q + 2 * B_KV
G_COLS = 2 * D_MODEL
LANES = 128
PAIRS_PER_STEP = 2
SAMPLES_PER_STEP = 2
UNIT_UNROLL = 4
Q_SCALE = HEAD_DIM ** -0.5
A_KEEP = tuple(min(w, SEQ) for w, _ in DILATIONS)

_VMEM_LIMIT = 56 * 1024 * 1024


def _cparams(*sem):
    return pltpu.CompilerParams(dimension_semantics=sem, vmem_limit_bytes=_VMEM_LIMIT)


def _mod_spec(mod, tm, rows_per_mod):
    mr = mod.shape[2]
    return pl.BlockSpec((6, None, mr, D_MODEL), lambda i, *_: (0, (i * tm) // rows_per_mod, 0, 0))


def _layer_norm(z, g, b):
    mu = jnp.mean(z, axis=-1, keepdims=True)
    zc = z - mu
    var = jnp.mean(zc * zc, axis=-1, keepdims=True)
    return zc * lax.rsqrt(var + LN_EPS) * g + b


def _adaln_kernel(c_ref, w_ref, b_ref, o_ref):
    c = c_ref[...]
    s = (c * jax.nn.sigmoid(c)).astype(BF16)
    o_ref[...] = jnp.dot(s, w_ref[...].astype(BF16), preferred_element_type=F32) + b_ref[...]


def _adaln(c_all, w_ada, b_ada):
    rows = c_all.shape[0]
    tn = 1024
    return pl.pallas_call(
        _adaln_kernel,
        grid=(DEPTH, 6 * D_MODEL // tn),
        in_specs=[
            pl.BlockSpec((rows, D_MODEL), lambda l, j: (0, 0)),
            pl.BlockSpec((None, D_MODEL, tn), lambda l, j: (l, 0, j)),
            pl.BlockSpec((None, 1, tn), lambda l, j: (l, 0, j)),
        ],
        out_specs=pl.BlockSpec((None, rows, tn), lambda l, j: (l, 0, j)),
        out_shape=jax.ShapeDtypeStruct((DEPTH, rows, 6 * D_MODEL), F32),
        compiler_params=_cparams("parallel", "parallel"),
        name="adaln",
    )(c_all, w_ada, b_ada.reshape(DEPTH, 1, 6 * D_MODEL))


def _modulate(x_ref, mod_ref, shift_idx, scale_idx):
    return (x_ref[...] * (1.0 + mod_ref[scale_idx]) + mod_ref[shift_idx]).astype(BF16)


def _proj_a_kernel(x_ref, mod_ref, w_ref, o_ref, u_ref):
    @pl.when(pl.program_id(1) == 0)
    def _():
        u_ref[...] = _modulate(x_ref, mod_ref, 0, 1)

    o_ref[...] = jnp.dot(u_ref[...], w_ref[...], preferred_element_type=F32)


def _proj_a(x, mod, w_a, l, tm, rows_per_mod):
    t = x.shape[0]
    return pl.pallas_call(
        _proj_a_kernel,
        grid=(t // tm, N_DIL),
        in_specs=[
            pl.BlockSpec((tm, D_MODEL), lambda i, g: (i, 0)),
            _mod_spec(mod, tm, rows_per_mod),
            pl.BlockSpec((None, D_MODEL, A_GROUP_COLS), lambda i, g: (l, 0, g)),
        ],
        out_specs=pl.BlockSpec((tm, A_GROUP_COLS), lambda i, g: (i, g)),
        out_shape=jax.ShapeDtypeStruct((t, A_COLS), F32),
        scratch_shapes=[pltpu.VMEM((tm, D_MODEL), BF16)],
        compiler_params=_cparams("parallel", "arbitrary"),
        name="proj_a",
    )(x, mod, w_a)


def _proj_b_kernel(x_ref, mod_ref, w_ref, cos_ref, sin_ref, q_ref, kv_ref):
    u = _modulate(x_ref, mod_ref, 0, 1)
    acc = jnp.dot(u, w_ref[...], preferred_element_type=F32)
    cos = cos_ref[...]
    sin = sin_ref[...]
    lane = lax.broadcasted_iota(jnp.int32, cos.shape, 1)
    first_half = (lane % HEAD_DIM) < (HEAD_DIM // 2)

    def rope(v):
        partner = jnp.where(first_half,
                            pltpu.roll(v, LANES - HEAD_DIM // 2, axis=1),
                            pltpu.roll(v, HEAD_DIM // 2, axis=1))
        return v * cos + partner * sin

    for c in range(B_Q // LANES):
        q_ref[:, c * LANES:(c + 1) * LANES] = rope(acc[:, c * LANES:(c + 1) * LANES]).astype(BF16)
    kv_ref[:, :B_KV] = rope(acc[:, B_Q:B_Q + B_KV])
    kv_ref[:, B_KV:] = acc[:, B_Q + B_KV:]


def _proj_b(x, mod, w_b, l, cos_t, sin_t, tm, rows_per_mod):
    t = x.shape[0]
    n_pos_blocks = cos_t.shape[0] // tm
    return pl.pallas_call(
        _proj_b_kernel,
        grid=(t // tm,),
        in_specs=[
            pl.BlockSpec((tm, D_MODEL), lambda i: (i, 0)),
            _mod_spec(mod, tm, rows_per_mod),
            pl.BlockSpec((None, D_MODEL, B_COLS), lambda i: (l, 0, 0)),
            pl.BlockSpec((tm, LANES), lambda i: (i % n_pos_blocks, 0)),
            pl.BlockSpec((tm, LANES), lambda i: (i % n_pos_blocks, 0)),
        ],
        out_specs=[
            pl.BlockSpec((tm, B_Q), lambda i: (i, 0)),
            pl.BlockSpec((tm, 2 * B_KV), lambda i: (i, 0)),
        ],
        out_shape=[
            jax.ShapeDtypeStruct((t, B_Q), BF16),
            jax.ShapeDtypeStruct((t, 2 * B_KV), F32),
        ],
        compiler_params=_cparams("parallel"),
        name="proj_b",
    )(x, mod, w_b, cos_t, sin_t)


def _gates_kernel(x_ref, mod_ref, w_ref, o_ref, u_ref):
    @pl.when(pl.program_id(1) == 0)
    def _():
        u_ref[...] = _modulate(x_ref, mod_ref, 0, 1)

    acc = jnp.dot(u_ref[...], w_ref[...], preferred_element_type=F32)
    o_ref[...] = jax.nn.sigmoid(acc).astype(o_ref.dtype)


def _gates(x, mod, w_g, l, tm, rows_per_mod):
    t = x.shape[0]
    tn = 1024
    return pl.pallas_call(
        _gates_kernel,
        grid=(t // tm, G_COLS // tn),
        in_specs=[
            pl.BlockSpec((tm, D_MODEL), lambda i, j: (i, 0)),
            _mod_spec(mod, tm, rows_per_mod),
            pl.BlockSpec((None, D_MODEL, tn), lambda i, j: (l, 0, j)),
        ],
        out_specs=pl.BlockSpec((tm, tn), lambda i, j: (i, j)),
        out_shape=jax.ShapeDtypeStruct((t, G_COLS), BF16),
        scratch_shapes=[pltpu.VMEM((tm, D_MODEL), BF16)],
        compiler_params=_cparams("parallel", "arbitrary"),
        name="gates",
    )(x, mod, w_g)


def _dot_nt(a, b):
    return lax.dot_general(a, b, (((1,), (1,)), ((), ())), preferred_element_type=F32)


def _band_masks():
    row = lax.broadcasted_iota(jnp.int32, (2 * BLK, BLK), 0) % BLK
    col = lax.broadcasted_iota(jnp.int32, (2 * BLK, BLK), 1)
    return col >= row, col <= row


def _rows(start, d):
    return pl.ds(start, BLK, stride=d) if d > 1 else pl.ds(pl.multiple_of(start, BLK), BLK)


def _stack_pair(q):
    lo = lax.broadcasted_iota(jnp.int32, (BLK, LANES), 1) < HEAD_DIM
    zero = jnp.zeros_like(q)
    return jnp.concatenate([jnp.where(lo, q, zero), jnp.where(lo, zero, q)], axis=0)


def _unstack_pair(x):
    lo = lax.broadcasted_iota(jnp.int32, (BLK, LANES), 1) < HEAD_DIM
    return jnp.where(lo, jnp.broadcast_to(x[:BLK], (BLK, LANES)), jnp.broadcast_to(x[BLK:], (BLK, LANES)))


def _pair_attend(qs, cur, prev, has_prev, sink):
    valid_p, valid_c = _band_masks()

    def scores(k, bias, valid):
        s = _dot_nt(qs, k)
        if bias is not None:
            s = s + bias
        return jnp.where(valid, s, NEG_INF)

    sc = scores(cur[0], cur[2], valid_c)
    if prev is not None:
        sp = jnp.where(has_prev, scores(prev[0], prev[2], valid_p), NEG_INF)
        m = jnp.max(jnp.maximum(sp, sc), axis=-1, keepdims=True)
    else:
        m = jnp.max(sc, axis=-1, keepdims=True)
    if sink is not None:
        m = jnp.maximum(m, sink)
    pc = jnp.exp(sc - m)
    pv = jnp.dot(pc.astype(BF16), cur[1], preferred_element_type=F32)
    if prev is not None:
        pp = jnp.exp(sp - m)
        pv = pv + jnp.dot(pp.astype(BF16), prev[1], preferred_element_type=F32)
        den = jnp.sum(pp + pc, axis=-1, keepdims=True)
    else:
        den = jnp.sum(pc, axis=-1, keepdims=True)
    if sink is not None:
        den = den + jnp.exp(sink - m)
    return _unstack_pair(pv), _unstack_pair(m), _unstack_pair(den)


def _dil_group(gi, q_refs, k_refs, v_refs, tp_ref, tc_ref, acc_ref, m_ref, l_ref):
    d = DILATIONS[gi][1]
    nblk = SEQ // (BLK * d)

    def unit(u, carry):
        r = u // nblk
        j = u % nblk
        start = j * (BLK * d) + r
        start_prev = jnp.maximum(j - 1, 0) * (BLK * d) + r
        rows = _rows(start, d)
        rows_prev = _rows(start_prev, d)
        for p in range(PAIRS_PER_STEP):
            qs = _stack_pair(q_refs[p][rows, :].astype(BF16))
            cur = (k_refs[p][rows, :].astype(BF16), v_refs[p][rows, :].astype(BF16),
                   tc_ref[2 * p:2 * p + 2].reshape(2 * BLK, BLK))
            prev = None
            if nblk > 1:
                prev = (k_refs[p][rows_prev, :].astype(BF16), v_refs[p][rows_prev, :].astype(BF16),
                        tp_ref[2 * p:2 * p + 2].reshape(2 * BLK, BLK))
            num, m_g, l_g = _pair_attend(qs, cur, prev, j > 0, None)
            if _group_step(gi) == 0:
                acc_ref[p, rows, :] = num
                m_ref[p, rows, :] = m_g
                l_ref[p, rows, :] = l_g
            else:
                m_old = m_ref[p, rows, :]
                m_new = jnp.maximum(m_old, m_g)
                a = jnp.exp(m_old - m_new)
                b = jnp.exp(m_g - m_new)
                acc_ref[p, rows, :] = acc_ref[p, rows, :] * a + num * b
                l_ref[p, rows, :] = l_ref[p, rows, :] * a + l_g * b
                m_ref[p, rows, :] = m_new
        return carry

    lax.fori_loop(0, SEQ // BLK, unit, 0, unroll=UNIT_UNROLL)


def _group_step(gi):
    return N_DIL - 1 - gi


def _dil_attn_kernel(q0, q1, k0, k1, v0, v1, tp_ref, tc_ref, o_ref, c0_ref, c1_ref, c2_ref, acc_ref, m_ref, l_ref):
    g = pl.program_id(2)
    q_refs, k_refs, v_refs = (q0, q1), (k0, k1), (v0, v1)
    c_refs = (c0_ref, c1_ref, c2_ref)
    for gi in range(N_DIL):
        @pl.when(g == _group_step(gi))
        def _(gi=gi):
            _dil_group(gi, q_refs, k_refs, v_refs, tp_ref, tc_ref, acc_ref, m_ref, l_ref)
            keep = A_KEEP[gi]
            for p in range(PAIRS_PER_STEP):
                c_refs[gi][0, 2 * p:2 * p + 2] = k_refs[p][SEQ - keep:, :].T.reshape(2, HEAD_DIM, keep)
                c_refs[gi][1, 2 * p:2 * p + 2] = v_refs[p][SEQ - keep:, :].T.reshape(2, HEAD_DIM, keep)

    @pl.when(g == N_DIL - 1)
    def _():
        for p in range(PAIRS_PER_STEP):
            o_ref[:, p * LANES:(p + 1) * LANES] = (acc_ref[p] / l_ref[p]).astype(BF16)


def _dil_attn(h_a, tps, tcs):
    hv = h_a.reshape(BATCH, SEQ, A_COLS)
    steps = A_HEADS // 2 // PAIRS_PER_STEP
    heads = 2 * PAIRS_PER_STEP
    slab = (None, SEQ, LANES)
    col = lambda part, p: (lambda b, s, g: (b, 0, _group_step(g) * 12 + part * 4 + s * PAIRS_PER_STEP + p))
    tbl = pl.BlockSpec((None, heads, BLK, BLK), lambda b, s, g: (_group_step(g), s, 0, 0))
    cache = lambda keep: pl.BlockSpec((None, 2, heads, HEAD_DIM, keep), lambda b, s, g: (b, 0, s, 0, 0))
    outs = pl.pallas_call(
        _dil_attn_kernel,
        grid=(BATCH, steps, N_DIL),
        in_specs=[pl.BlockSpec(slab, col(part, p)) for part in range(3) for p in range(PAIRS_PER_STEP)] + [tbl, tbl],
        out_specs=[pl.BlockSpec((None, SEQ, heads * HEAD_DIM), lambda b, s, g: (b, 0, s))]
        + [cache(keep) for keep in A_KEEP],
        out_shape=[jax.ShapeDtypeStruct((BATCH, SEQ, A_HD), BF16)]
        + [jax.ShapeDtypeStruct((BATCH, 2, A_HEADS, HEAD_DIM, keep), F32) for keep in A_KEEP],
        scratch_shapes=[pltpu.VMEM((PAIRS_PER_STEP, SEQ, LANES), F32)] * 3,
        compiler_params=_cparams("parallel", "parallel", "arbitrary"),
        name="dil_attn",
    )(*([hv] * (3 * PAIRS_PER_STEP)), tps, tcs)
    return outs[0].reshape(BATCH * SEQ, A_HD), outs[1:]


def _band_b_kernel(sink_ref, q_ref, kvp_ref, kvc_ref, o_ref, c_ref):
    j = pl.program_id(1)
    lo = lax.broadcasted_iota(jnp.int32, (BLK, LANES), 1) < HEAD_DIM
    first = lax.broadcasted_iota(jnp.int32, (2 * BLK, 1), 0) < BLK

    def per_kv_head(x):
        rolled = pltpu.roll(x, HEAD_DIM, axis=1)
        return jnp.where(lo, x, rolled).astype(BF16), jnp.where(lo, rolled, x).astype(BF16)

    kp, kc = per_kv_head(kvp_ref[:, :B_KV]), per_kv_head(kvc_ref[:, :B_KV])
    vp, vc = per_kv_head(kvp_ref[:, B_KV:]), per_kv_head(kvc_ref[:, B_KV:])
    for pair in range(B_HEADS // 2):
        hk = (2 * pair) // B_GROUP
        sl = slice(pair * LANES, (pair + 1) * LANES)
        sink = jnp.where(first, sink_ref[2 * pair], sink_ref[2 * pair + 1])
        num, _, den = _pair_attend(_stack_pair(q_ref[:, sl]), (kc[hk], vc[hk], None), (kp[hk], vp[hk], None),
                                   j > 0, sink)
        o_ref[:, sl] = (num / den).astype(BF16)

    @pl.when(j == pl.num_programs(1) - 1)
    def _():
        c_ref[...] = kvc_ref[...].T


def _band_b(qb, kvb, sink):
    nb = SEQ // BLK
    qv = qb.reshape(BATCH, SEQ, B_Q)
    kvv = kvb.reshape(BATCH, SEQ, 2 * B_KV)
    prev = lambda j: jnp.maximum(j - 1, 0)
    o, c = pl.pallas_call(
        _band_b_kernel,
        grid=(BATCH, nb),
        in_specs=[
            pl.BlockSpec(memory_space=pltpu.SMEM),
            pl.BlockSpec((None, BLK, B_Q), lambda b, j: (b, j, 0)),
            pl.BlockSpec((None, BLK, 2 * B_KV), lambda b, j: (b, prev(j), 0)),
            pl.BlockSpec((None, BLK, 2 * B_KV), lambda b, j: (b, j, 0)),
        ],
        out_specs=[
            pl.BlockSpec((None, BLK, B_Q), lambda b, j: (b, j, 0)),
            pl.BlockSpec((None, 2 * B_KV, BLK), lambda b, j: (b, 0, 0)),
        ],
        out_shape=[
            jax.ShapeDtypeStruct((BATCH, SEQ, B_Q), BF16),
            jax.ShapeDtypeStruct((BATCH, 2 * B_KV, BLK), F32),
        ],
        compiler_params=_cparams("parallel", "arbitrary"),
        name="band_b",
    )(sink, qv, kvv, kvv)
    return o.reshape(BATCH * SEQ, B_Q), c


_N_HEADS = N_DIL * A_HEADS + B_HEADS
_COL_Q, _COL_K, _COL_V = 0, _N_HEADS, 2 * _N_HEADS
_COL_SINK_Q = N_DIL * A_HEADS


def _sample_attn_kernel(cols_ref, c0_ref, c1_ref, c2_ref, cb_ref, b0_ref, b1_ref, b2_ref, bnew_ref, sink_ref, o_ref):
    ns = SAMPLES_PER_STEP
    caches = (c0_ref, c1_ref, c2_ref)
    biases = (b0_ref, b1_ref, b2_ref)
    slab = cols_ref[...]
    k_new = pltpu.roll(slab, LANES - _COL_K, axis=2)
    v_new = pltpu.roll(slab, LANES - _COL_V, axis=2)
    s_new_row = jnp.sum(slab * k_new, axis=1, keepdims=True) + bnew_ref[...]
    lane_row = lax.broadcasted_iota(jnp.int32, (1, LANES), 1)
    lane = lax.broadcasted_iota(jnp.int32, (HEAD_DIM, LANES), 1)

    def per_head(x, start, count):
        return jnp.stack([jnp.stack([x[s, :, start + i:start + i + 1] for i in range(count)]) for s in range(ns)])

    def to_row(vals, start, rows):
        for s in range(ns):
            for i in range(vals.shape[1]):
                rows[s] = jnp.where(lane_row == start + i, vals[s, i], rows[s])

    def attend(start, count, kt_ref, vt_ref, bias_ref, sink):
        q = per_head(slab, start, count)
        s_new = per_head(s_new_row, start, count)
        tiles = []
        m_t = None
        for t in range(kt_ref.shape[-1] // LANES):
            sl = slice(t * LANES, (t + 1) * LANES)
            s = jnp.sum(kt_ref[:, :, :, sl] * q, axis=2, keepdims=True)
            if bias_ref is not None:
                bias = bias_ref[:, :, sl]
                s = jnp.where(bias > 0.5 * NEG_INF, s + bias, NEG_INF)
            tiles.append(s)
            m_t = s if m_t is None else jnp.maximum(m_t, s)
        m = jnp.maximum(jnp.max(m_t, axis=-1, keepdims=True), s_new)
        if sink is not None:
            m = jnp.maximum(m, sink)
        p_new = jnp.exp(s_new - m)
        den_t = jnp.zeros_like(m_t)
        pv_t = jnp.zeros((ns, count, HEAD_DIM, LANES), F32)
        for t, s in enumerate(tiles):
            p = jnp.exp(s - m)
            den_t = den_t + p
            pv_t = pv_t + vt_ref[:, :, :, t * LANES:(t + 1) * LANES] * p
        den = jnp.sum(den_t, axis=-1, keepdims=True) + p_new
        if sink is not None:
            den = den + jnp.exp(sink - m)
        return jnp.sum(pv_t, axis=-1, keepdims=True), m, p_new, den

    parts = [attend(g * A_HEADS, A_HEADS, caches[g].at[:, 0], caches[g].at[:, 1], biases[g], None)
             for g in range(N_DIL)]
    lses = [m + jnp.log(den) for _, m, _, den in parts]
    mx = jnp.maximum(jnp.maximum(lses[0], lses[1]), lses[2])
    es = [jnp.exp(lse - mx) for lse in lses]
    total = es[0] + es[1] + es[2]
    cache_a = jnp.zeros((ns, A_HEADS, HEAD_DIM, 1), F32)
    coef_rows = [jnp.zeros((1, LANES), F32) for _ in range(ns)]
    for g, (pv, _, p_new, den) in enumerate(parts):
        coef = es[g] / (total * den)
        cache_a = cache_a + coef * pv
        to_row(coef * p_new, g * A_HEADS, coef_rows)
    cache_b = []
    for hk in range(B_KV_HEADS):
        start = _COL_SINK_Q + hk * B_GROUP
        pv, _, p_new, den = attend(start, B_GROUP, cb_ref.at[:, 0, hk:hk + 1], cb_ref.at[:, 1, hk:hk + 1], None,
                                   sink_ref[hk * B_GROUP:(hk + 1) * B_GROUP])
        cache_b.append(pv / den)
        to_row(p_new / den, start, coef_rows)

    shift = lambda x, k: pltpu.roll(x, LANES - k, axis=1)
    for s in range(ns):
        out = jnp.zeros((HEAD_DIM, LANES), F32)
        for h in range(A_HEADS):
            out = jnp.where(lane == h, cache_a[s, h], out)
        for hq in range(B_HEADS):
            out = jnp.where(lane == A_HEADS + hq, cache_b[hq // B_GROUP][s, hq % B_GROUP], out)
        new = v_new[s] * coef_rows[s]
        groups = new + shift(new, A_HEADS) + shift(new, 2 * A_HEADS)
        o_ref[s] = out + jnp.where(lane < A_HEADS, groups, shift(new, 2 * A_HEADS))


def _sample_attn(h_a, qb, kvb, caches_t, l, bias_lanes, bias_new, sink):
    n = DEC_BATCH
    ha = h_a.reshape(n, N_DIL, 3, A_HEADS, HEAD_DIM)
    kb = jnp.repeat(kvb.reshape(n, 2, B_KV_HEADS, HEAD_DIM), B_GROUP, axis=2)
    flat = lambda part: ha[:, :, part].reshape(n, N_DIL * A_HEADS, HEAD_DIM)
    cols = jnp.concatenate([
        flat(0), qb.astype(F32).reshape(n, B_HEADS, HEAD_DIM), flat(1), kb[:, 0], flat(2), kb[:, 1],
        jnp.zeros((n, LANES - 3 * _N_HEADS, HEAD_DIM), F32)], axis=1)
    cols = cols.transpose(0, 2, 1)
    cache_spec = lambda c: pl.BlockSpec((None, SAMPLES_PER_STEP) + c.shape[2:], lambda b: (l, b, 0, 0, 0, 0))
    full = lambda a: pl.BlockSpec(a.shape, lambda b: (0,) * a.ndim)
    small = [a[:, None, :] for a in bias_lanes] + [bias_new, sink.reshape(B_HEADS, 1, 1)]
    out = pl.pallas_call(
        _sample_attn_kernel,
        grid=(n // SAMPLES_PER_STEP,),
        in_specs=[pl.BlockSpec((SAMPLES_PER_STEP, HEAD_DIM, LANES), lambda b: (b, 0, 0))]
        + [cache_spec(c) for c in caches_t] + [full(a) for a in small],
        out_specs=pl.BlockSpec((SAMPLES_PER_STEP, HEAD_DIM, LANES), lambda b: (b, 0, 0)),
        out_shape=jax.ShapeDtypeStruct((n, HEAD_DIM, LANES), F32),
        compiler_params=_cparams("parallel"),
        name="sample_attn",
    )(cols, *caches_t, *small)
    heads = out.transpose(0, 2, 1)
    out_a = heads[:, :A_HEADS].reshape(n, A_HD)
    out_b = heads[:, A_HEADS:A_HEADS + B_HEADS].reshape(n, B_Q)
    return out_a.astype(BF16), out_b.astype(BF16)


def _post_attn_kernel(a_ref, ob_ref, gate_ref, x_ref, mod_ref, wpa_ref, wpb_ref, wo_ref, lng_ref, lnb_ref, o_ref):
    ya = jnp.dot(a_ref[...], wpa_ref[...], preferred_element_type=F32)
    yb = jnp.dot(ob_ref[...], wpb_ref[...], preferred_element_type=F32)
    mix = gate_ref[:, :D_MODEL].astype(F32) * ya + gate_ref[:, D_MODEL:].astype(F32) * yb
    y = jnp.dot(mix.astype(BF16), wo_ref[...], preferred_element_type=F32)
    z = ALPHA * x_ref[...] + mod_ref[2] * y
    o_ref[...] = _layer_norm(z, lng_ref[0:1, :], lnb_ref[0:1, :])


def _post_attn(oa, ob, gates, x, mod, w_pa, w_pb, w_o, ln_g, ln_b, l, tm, rows_per_mod):
    t = x.shape[0]
    row = lambda w: pl.BlockSpec((tm, w), lambda i: (i, 0))
    const = lambda shape: pl.BlockSpec((None,) + shape, lambda i: (l, 0, 0), pipeline_mode=pl.Buffered(1))
    return pl.pallas_call(
        _post_attn_kernel,
        grid=(t // tm,),
        in_specs=[
            row(A_HD), row(B_Q), row(G_COLS), row(D_MODEL),
            _mod_spec(mod, tm, rows_per_mod),
            const((A_HD, D_MODEL)), const((B_Q, D_MODEL)), const((D_MODEL, D_MODEL)),
            const((2, D_MODEL)), const((2, D_MODEL)),
        ],
        out_specs=row(D_MODEL),
        out_shape=jax.ShapeDtypeStruct((t, D_MODEL), F32),
        compiler_params=_cparams("parallel"),
        name="post_attn",
    )(oa, ob, gates, x, mod, w_pa, w_pb, w_o, ln_g, ln_b)


def _ffn_kernel(x_ref, mod_ref, wg_ref, wu_ref, wd_ref, lng_ref, lnb_ref, o_ref, u_ref, acc_ref):
    k = pl.program_id(1)

    @pl.when(k == 0)
    def _():
        u_ref[...] = _modulate(x_ref, mod_ref, 3, 4)
        acc_ref[...] = jnp.zeros_like(acc_ref)

    u = u_ref[...]
    gate = jnp.dot(u, wg_ref[...], preferred_element_type=F32)
    up = jnp.dot(u, wu_ref[...], preferred_element_type=F32)
    act = (gate * jax.nn.sigmoid(gate) * up).astype(BF16)
    acc_ref[...] += jnp.dot(act, wd_ref[...], preferred_element_type=F32)

    @pl.when(k == pl.num_programs(1) - 1)
    def _():
        z = ALPHA * x_ref[...] + mod_ref[5] * acc_ref[...]
        o_ref[...] = _layer_norm(z, lng_ref[1:2, :], lnb_ref[1:2, :])


def _ffn(x, mod, w_gu, w_down, ln_g, ln_b, l, tm, rows_per_mod):
    t = x.shape[0]
    tf = 512
    nf = D_FF // tf
    return pl.pallas_call(
        _ffn_kernel,
        grid=(t // tm, nf),
        in_specs=[
            pl.BlockSpec((tm, D_MODEL), lambda i, k: (i, 0)),
            _mod_spec(mod, tm, rows_per_mod),
            pl.BlockSpec((None, D_MODEL, tf), lambda i, k: (l, 0, k)),
            pl.BlockSpec((None, D_MODEL, tf), lambda i, k: (l, 0, k + nf)),
            pl.BlockSpec((None, tf, D_MODEL), lambda i, k: (l, k, 0)),
            pl.BlockSpec((None, 2, D_MODEL), lambda i, k: (l, 0, 0)),
            pl.BlockSpec((None, 2, D_MODEL), lambda i, k: (l, 0, 0)),
        ],
        out_specs=pl.BlockSpec((tm, D_MODEL), lambda i, k: (i, 0)),
        out_shape=jax.ShapeDtypeStruct((t, D_MODEL), F32),
        scratch_shapes=[pltpu.VMEM((tm, D_MODEL), BF16), pltpu.VMEM((tm, D_MODEL), F32)],
        compiler_params=_cparams("parallel", "arbitrary"),
        name="ffn",
    )(x, mod, w_gu, w_gu, w_down, ln_g, ln_b)


def _t5_bucket(dist):
    exact = N_BUCKETS // 2
    n = jnp.maximum(dist, 0)
    log_ratio = jnp.log(jnp.maximum(n, exact).astype(F32) / exact) / math.log(T5_MAX_DIST / exact)
    large = jnp.minimum(exact + (log_ratio * (N_BUCKETS - exact)).astype(jnp.int32), N_BUCKETS - 1)
    return jnp.where(n < exact, n, large)


def _bias_tables(t5_table):
    tps, tcs, lanes, news = [], [], [], []
    for g, (w, d) in enumerate(DILATIONS):
        bucket = _t5_bucket(d * jnp.arange(BLK + 1))
        onehot = (bucket[:, None] == jnp.arange(N_BUCKETS)[None, :]).astype(F32)
        bias = jnp.dot(onehot, t5_table[:, g * A_HEADS:(g + 1) * A_HEADS],
                       precision=lax.Precision.HIGHEST).T
        wv = jnp.concatenate([bias[:, ::-1], jnp.zeros((A_HEADS, BLK), F32)], axis=1)
        toep = jnp.tile(wv, (1, BLK))[:, :BLK * 2 * BLK].reshape(A_HEADS, BLK, 2 * BLK)
        tps.append(toep[:, :, :BLK])
        tcs.append(toep[:, :, BLK:])
        by_row = jnp.repeat(bias[:, BLK:0:-1], d, axis=1)
        t = jnp.arange(BLK * d)[None, :]
        lanes.append(jnp.where(t % d == 0, by_row, NEG_INF))
        news.append(bias[:, 0])
    new = jnp.concatenate(news + [jnp.zeros((LANES - N_DIL * A_HEADS,), F32)])[None, :]
    return jnp.stack(tps), jnp.stack(tcs), lanes, new


def _rope_tables(pos):
    half = HEAD_DIM // 2
    inv = ROPE_THETA ** (-jnp.arange(half, dtype=F32) / half)
    ang = pos.astype(F32)[:, None] * inv[None]
    cos, sin = jnp.cos(ang), jnp.sin(ang)
    cos_t = jnp.concatenate([cos, cos, cos, cos], axis=-1)
    sin_t = jnp.concatenate([-sin, sin, -sin, sin], axis=-1)
    return cos_t, sin_t


def kernel(x_prompt, x_sample, c_prompt, c_sample, cache_a0, cache_a1, cache_a2, cache_b, t5_table, w_ada, b_ada,
           w_in, sinks, w_pa, w_pb, w_o, w_gu, w_down, ln_g, ln_b):
    tp_rows = BATCH * SEQ
    mod = _adaln(jnp.concatenate([c_prompt, c_sample], axis=0), w_ada, b_ada)
    mod_p = mod[:, :BATCH].reshape(DEPTH, BATCH, 6, 1, D_MODEL).transpose(0, 2, 1, 3, 4)
    mod_s = mod[:, BATCH:].reshape(DEPTH, 1, DEC_BATCH, 6, D_MODEL).transpose(0, 3, 1, 2, 4)

    q_scale = jnp.concatenate([jnp.full((A_HD,), Q_SCALE, F32), jnp.ones((2 * A_HD,), F32)])
    w_a = (w_in[:, :, :A_COLS] * jnp.tile(q_scale, N_DIL)).astype(BF16)
    b_scale = jnp.concatenate([jnp.full((B_Q,), Q_SCALE, F32), jnp.ones((2 * B_KV,), F32)])
    w_b = (w_in[:, :, A_COLS:A_COLS + B_COLS] * b_scale).astype(BF16)
    w_g = w_in[:, :, A_COLS + B_COLS:].astype(BF16)
    w_pa_h, w_pb_h, w_o_h = w_pa.astype(BF16), w_pb.astype(BF16), w_o.astype(BF16)
    w_gu_h, w_down_h = w_gu.astype(BF16), w_down.astype(BF16)

    tps, tcs, bias_lanes, bias_new = _bias_tables(t5_table)
    cos_p, sin_p = _rope_tables(jnp.arange(SEQ))
    cos_s, sin_s = _rope_tables(jnp.full((DEC_BATCH,), PAST_LEN))
    caches_t = [c.transpose(0, 1, 3, 4, 5, 2) for c in (cache_a0, cache_a1, cache_a2, cache_b)]

    xp = x_prompt.reshape(tp_rows, D_MODEL)
    xs = x_sample.reshape(DEC_BATCH, D_MODEL)
    rows_a_p = [[] for _ in range(N_DIL)]
    rows_a_s = [[] for _ in range(N_DIL)]
    rows_b_p, rows_b_s = [], []
    for l in range(DEPTH):
        sink = sinks[l].reshape(B_HEADS)

        h_a = _proj_a(xp, mod_p[l], w_a, l, 1024, SEQ)
        qb, kvb = _proj_b(xp, mod_p[l], w_b, l, cos_p, sin_p, 512, SEQ)
        gates = _gates(xp, mod_p[l], w_g, l, 1024, SEQ)
        oa, kv_rows = _dil_attn(h_a, tps, tcs)
        ob, b_rows = _band_b(qb, kvb, sink)
        xp = _post_attn(oa, ob, gates, xp, mod_p[l], w_pa_h, w_pb_h, w_o_h, ln_g, ln_b, l, 256, SEQ)
        xp = _ffn(xp, mod_p[l], w_gu_h, w_down_h, ln_g, ln_b, l, 512, SEQ)
        for g in range(N_DIL):
            rows_a_p[g].append(kv_rows[g])
        rows_b_p.append(b_rows.reshape(BATCH, 2, B_KV_HEADS, HEAD_DIM, BLK))

        h_a = _proj_a(xs, mod_s[l], w_a, l, DEC_BATCH, DEC_BATCH)
        qb, kvb = _proj_b(xs, mod_s[l], w_b, l, cos_s, sin_s, DEC_BATCH, DEC_BATCH)
        gates = _gates(xs, mod_s[l], w_g, l, DEC_BATCH, DEC_BATCH)
        oa, ob = _sample_attn(h_a, qb, kvb, caches_t, l, bias_lanes, bias_new, sink)
        xs = _post_attn(oa, ob, gates, xs, mod_s[l], w_pa_h, w_pb_h, w_o_h, ln_g, ln_b, l, DEC_BATCH, DEC_BATCH)
        xs = _ffn(xs, mod_s[l], w_gu_h, w_down_h, ln_g, ln_b, l, DEC_BATCH, DEC_BATCH)
        ha = h_a.reshape(DEC_BATCH, 1, N_DIL, 3, A_HEADS, HEAD_DIM)
        for g in range(N_DIL):
            rows_a_s[g].append(ha[:, :, g, 1:])
        rows_b_s.append(kvb.reshape(DEC_BATCH, 1, 2, B_KV_HEADS, HEAD_DIM))

    to_rows_major = lambda parts: jnp.stack(parts).transpose(0, 1, 5, 2, 3, 4)
    outs = [xp.reshape(BATCH, SEQ, D_MODEL), xs.reshape(DEC_BATCH, 1, D_MODEL)]
    for g in range(N_DIL):
        outs.append(to_rows_major(rows_a_p[g]))
        outs.append(jnp.stack(rows_a_s[g]))
    outs.append(to_rows_major(rows_b_p))
    outs.append(jnp.stack(rows_b_s))
    return tuple(outs)
```

```python
import functools
import math

import jax
import jax.numpy as jnp
from jax import lax
from jax.experimental import pallas as pl
from jax.experimental.pallas import tpu as pltpu

F32 = jnp.float32
BF16 = jnp.bfloat16

D_MODEL = 2048
BATCH = 8
SEQ = 2048
DEPTH = 2
DEC_BATCH = 128
PAST_LEN = 8192
HEAD_DIM = 64
DILATIONS = ((128, 1), (512, 4), (2048, 16))
N_DIL = 3
A_HEADS = 8
B_HEADS = 16
B_KV_HEADS = 2
B_GROUP = B_HEADS // B_KV_HEADS
B_WINDOW = 128
ROPE_THETA = 150000.0
N_BUCKETS = 32
T5_MAX_DIST = 2048
D_FF = ((8 * D_MODEL + 3 * 256 - 1) // (3 * 256)) * 256
ALPHA = (2 * DEPTH) ** 0.25
LN_EPS = 1e-5
NEG_INF = -1e30

BLK = 128
A_HD = A_HEADS * HEAD_DIM
A_GROUP_COLS = 3 * A_HD
A_COLS = N_DIL * A_GROUP_COLS
B_Q = B_HEADS * HEAD_DIM
B_KV = B_KV_HEADS * HEAD_DIM
B_COLS = B_Q + 2 * B_KV
G_COLS = 2 * D_MODEL
LANES = 128
PAIRS_PER_STEP = 2
SAMPLES_PER_STEP = 2
UNIT_UNROLL = 4
Q_SCALE = HEAD_DIM ** -0.5
A_KEEP = tuple(min(w, SEQ) for w, _ in DILATIONS)

_VMEM_LIMIT = 56 * 1024 * 1024


def _cparams(*sem):
    return pltpu.CompilerParams(dimension_semantics=sem, vmem_limit_bytes=_VMEM_LIMIT)


def _mod_spec(mod, tm, rows_per_mod):
    mr = mod.shape[2]
    return pl.BlockSpec((6, None, mr, D_MODEL), lambda i, *_: (0, (i * tm) // rows_per_mod, 0, 0))


def _layer_norm(z, g, b):
    mu = jnp.mean(z, axis=-1, keepdims=True)
    zc = z - mu
    var = jnp.mean(zc * zc, axis=-1, keepdims=True)
    return zc * lax.rsqrt(var + LN_EPS) * g + b


def _adaln_kernel(c_ref, w_ref, b_ref, o_ref):
    c = c_ref[...]
    s = (c * jax.nn.sigmoid(c)).astype(BF16)
    o_ref[...] = jnp.dot(s, w_ref[...].astype(BF16), preferred_element_type=F32) + b_ref[...]


def _adaln(c_all, w_ada, b_ada):
    rows = c_all.shape[0]
    tn = 1024
    return pl.pallas_call(
        _adaln_kernel,
        grid=(DEPTH, 6 * D_MODEL // tn),
        in_specs=[
            pl.BlockSpec((rows, D_MODEL), lambda l, j: (0, 0)),
            pl.BlockSpec((None, D_MODEL, tn), lambda l, j: (l, 0, j)),
            pl.BlockSpec((None, 1, tn), lambda l, j: (l, 0, j)),
        ],
        out_specs=pl.BlockSpec((None, rows, tn), lambda l, j: (l, 0, j)),
        out_shape=jax.ShapeDtypeStruct((DEPTH, rows, 6 * D_MODEL), F32),
        compiler_params=_cparams("parallel", "parallel"),
        name="adaln",
    )(c_all, w_ada, b_ada.reshape(DEPTH, 1, 6 * D_MODEL))


def _modulate(x_ref, mod_ref, shift_idx, scale_idx):
    return (x_ref[...] * (1.0 + mod_ref[scale_idx]) + mod_ref[shift_idx]).astype(BF16)


def _proj_a_kernel(x_ref, mod_ref, w_ref, o_ref, u_ref):
    @pl.when(pl.program_id(1) == 0)
    def _():
        u_ref[...] = _modulate(x_ref, mod_ref, 0, 1)

    o_ref[...] = jnp.dot(u_ref[...], w_ref[...], preferred_element_type=F32)


def _proj_a(x, mod, w_a, l, tm, rows_per_mod):
    t = x.shape[0]
    return pl.pallas_call(
        _proj_a_kernel,
        grid=(t // tm, N_DIL),
        in_specs=[
            pl.BlockSpec((tm, D_MODEL), lambda i, g: (i, 0)),
            _mod_spec(mod, tm, rows_per_mod),
            pl.BlockSpec((None, D_MODEL, A_GROUP_COLS), lambda i, g: (l, 0, g)),
        ],
        out_specs=pl.BlockSpec((tm, A_GROUP_COLS), lambda i, g: (i, g)),
        out_shape=jax.ShapeDtypeStruct((t, A_COLS), F32),
        scratch_shapes=[pltpu.VMEM((tm, D_MODEL), BF16)],
        compiler_params=_cparams("parallel", "arbitrary"),
        name="proj_a",
    )(x, mod, w_a)


def _proj_b_kernel(x_ref, mod_ref, w_ref, cos_ref, sin_ref, q_ref, kv_ref):
    u = _modulate(x_ref, mod_ref, 0, 1)
    acc = jnp.dot(u, w_ref[0], preferred_element_type=F32)
    cos = cos_ref[...]
    sin = sin_ref[...]
    lane = lax.broadcasted_iota(jnp.int32, cos.shape, 1)
    first_half = (lane % HEAD_DIM) < (HEAD_DIM // 2)

    def rope(v):
        partner = jnp.where(first_half,
                            pltpu.roll(v, LANES - HEAD_DIM // 2, axis=1),
                            pltpu.roll(v, HEAD_DIM // 2, axis=1))
        return v * cos + partner * sin

    for c in range(B_Q // LANES):
        q_ref[:, c * LANES:(c + 1) * LANES] = rope(acc[:, c * LANES:(c + 1) * LANES]).astype(BF16)
    kv_ref[:, :B_KV] = rope(acc[:, B_Q:B_Q + B_KV])
    kv_ref[:, B_KV:] = acc[:, B_Q + B_KV:]


def _proj_b(x, mod, w_b, l, cos_t, sin_t, tm, rows_per_mod):
    t = x.shape[0]
    n_pos_blocks = cos_t.shape[0] // tm
    return pl.pallas_call(
        _proj_b_kernel,
        grid=(t // tm,),
        in_specs=[
            pl.BlockSpec((tm, D_MODEL), lambda i: (i, 0)),
            _mod_spec(mod, tm, rows_per_mod),
            pl.BlockSpec((pl.Element(1), pl.Element(D_MODEL), pl.Element(B_COLS)), lambda i: (l, 0, A_COLS)),
            pl.BlockSpec((tm, LANES), lambda i: (i % n_pos_blocks, 0)),
            pl.BlockSpec((tm, LANES), lambda i: (i % n_pos_blocks, 0)),
        ],
        out_specs=[
            pl.BlockSpec((tm, B_Q), lambda i: (i, 0)),
            pl.BlockSpec((tm, 2 * B_KV), lambda i: (i, 0)),
        ],
        out_shape=[
            jax.ShapeDtypeStruct((t, B_Q), BF16),
            jax.ShapeDtypeStruct((t, 2 * B_KV), F32),
        ],
        compiler_params=_cparams("parallel"),
        name="proj_b",
    )(x, mod, w_b, cos_t, sin_t)


def _gates_kernel(x_ref, mod_ref, w_ref, o_ref, u_ref):
    @pl.when(pl.program_id(1) == 0)
    def _():
        u_ref[...] = _modulate(x_ref, mod_ref, 0, 1)

    acc = jnp.dot(u_ref[...], w_ref[0], preferred_element_type=F32)
    o_ref[...] = jax.nn.sigmoid(acc).astype(o_ref.dtype)


def _gates(x, mod, w_g, l, tm, rows_per_mod):
    t = x.shape[0]
    tn = 1024
    return pl.pallas_call(
        _gates_kernel,
        grid=(t // tm, G_COLS // tn),
        in_specs=[
            pl.BlockSpec((tm, D_MODEL), lambda i, j: (i, 0)),
            _mod_spec(mod, tm, rows_per_mod),
            pl.BlockSpec((pl.Element(1), pl.Element(D_MODEL), pl.Element(tn)),
                         lambda i, j: (l, 0, pl.multiple_of(A_COLS + B_COLS + j * tn, LANES))),
        ],
        out_specs=pl.BlockSpec((tm, tn), lambda i, j: (i, j)),
        out_shape=jax.ShapeDtypeStruct((t, G_COLS), BF16),
        scratch_shapes=[pltpu.VMEM((tm, D_MODEL), BF16)],
        compiler_params=_cparams("parallel", "arbitrary"),
        name="gates",
    )(x, mod, w_g)


def _dot_nt(a, b):
    return lax.dot_general(a, b, (((1,), (1,)), ((), ())), preferred_element_type=F32)


def _band_masks():
    row = lax.broadcasted_iota(jnp.int32, (2 * BLK, BLK), 0) % BLK
    col = lax.broadcasted_iota(jnp.int32, (2 * BLK, BLK), 1)
    return col >= row, col <= row


def _rows(start, d):
    return pl.ds(start, BLK, stride=d) if d > 1 else pl.ds(pl.multiple_of(start, BLK), BLK)


def _stack_pair(q):
    lo = lax.broadcasted_iota(jnp.int32, (BLK, LANES), 1) < HEAD_DIM
    zero = jnp.zeros_like(q)
    return jnp.concatenate([jnp.where(lo, q, zero), jnp.where(lo, zero, q)], axis=0)


def _unstack_pair(x):
    lo = lax.broadcasted_iota(jnp.int32, (BLK, LANES), 1) < HEAD_DIM
    return jnp.where(lo, jnp.broadcast_to(x[:BLK], (BLK, LANES)), jnp.broadcast_to(x[BLK:], (BLK, LANES)))


def _pair_attend(qs, cur, prev, has_prev, sink):
    valid_p, valid_c = _band_masks()

    def scores(k, bias, valid):
        s = _dot_nt(qs, k)
        if bias is not None:
            s = s + bias
        return jnp.where(valid, s, NEG_INF)

    sc = scores(cur[0], cur[2], valid_c)
    if prev is not None:
        sp = jnp.where(has_prev, scores(prev[0], prev[2], valid_p), NEG_INF)
        m = jnp.max(jnp.maximum(sp, sc), axis=-1, keepdims=True)
    else:
        m = jnp.max(sc, axis=-1, keepdims=True)
    if sink is not None:
        m = jnp.maximum(m, sink)
    pc = jnp.exp(sc - m)
    pv = jnp.dot(pc.astype(BF16), cur[1], preferred_element_type=F32)
    if prev is not None:
        pp = jnp.exp(sp - m)
        pv = pv + jnp.dot(pp.astype(BF16), prev[1], preferred_element_type=F32)
        den = jnp.sum(pp + pc, axis=-1, keepdims=True)
    else:
        den = jnp.sum(pc, axis=-1, keepdims=True)
    if sink is not None:
        den = den + jnp.exp(sink - m)
    return _unstack_pair(pv), _unstack_pair(m), _unstack_pair(den)


def _dil_group(gi, q_refs, k_refs, v_refs, tp_ref, tc_ref, acc_ref, m_ref, l_ref):
    d = DILATIONS[gi][1]
    nblk = SEQ // (BLK * d)

    def unit(u, carry):
        r = u // nblk
        j = u % nblk
        start = j * (BLK * d) + r
        start_prev = jnp.maximum(j - 1, 0) * (BLK * d) + r
        rows = _rows(start, d)
        rows_prev = _rows(start_prev, d)
        for p in range(PAIRS_PER_STEP):
            qs = _stack_pair(q_refs[p][rows, :].astype(BF16))
            cur = (k_refs[p][rows, :].astype(BF16), v_refs[p][rows, :].astype(BF16),
                   tc_ref[2 * p:2 * p + 2].reshape(2 * BLK, BLK))
            prev = None
            if nblk > 1:
                prev = (k_refs[p][rows_prev, :].astype(BF16), v_refs[p][rows_prev, :].astype(BF16),
                        tp_ref[2 * p:2 * p + 2].reshape(2 * BLK, BLK))
            num, m_g, l_g = _pair_attend(qs, cur, prev, j > 0, None)
            if _group_step(gi) == 0:
                acc_ref[p, rows, :] = num
                m_ref[p, rows, :] = m_g
                l_ref[p, rows, :] = l_g
            else:
                m_old = m_ref[p, rows, :]
                m_new = jnp.maximum(m_old, m_g)
                a = jnp.exp(m_old - m_new)
                b = jnp.exp(m_g - m_new)
                acc_ref[p, rows, :] = acc_ref[p, rows, :] * a + num * b
                l_ref[p, rows, :] = l_ref[p, rows, :] * a + l_g * b
                m_ref[p, rows, :] = m_new
        return carry

    lax.fori_loop(0, SEQ // BLK, unit, 0, unroll=UNIT_UNROLL)


def _group_step(gi):
    return N_DIL - 1 - gi


def _dil_attn_kernel(q0, q1, k0, k1, v0, v1, tp_ref, tc_ref, *rest, layer, first):
    o_ref, c0_ref, c1_ref, c2_ref, acc_ref, m_ref, l_ref = rest[-7:]
    g = pl.program_id(2)
    q_refs, k_refs, v_refs = (q0, q1), (k0, k1), (v0, v1)
    c_refs = (c0_ref, c1_ref, c2_ref)
    for gi in range(N_DIL):
        @pl.when(g == _group_step(gi))
        def _(gi=gi):
            _dil_group(gi, q_refs, k_refs, v_refs, tp_ref, tc_ref, acc_ref, m_ref, l_ref)
            keep = A_KEEP[gi]
            c_ref = c_refs[gi].at[layer] if first else c_refs[gi]
            for p in range(PAIRS_PER_STEP):
                c_ref[0, 2 * p:2 * p + 2] = k_refs[p][SEQ - keep:, :].T.reshape(2, HEAD_DIM, keep)
                c_ref[1, 2 * p:2 * p + 2] = v_refs[p][SEQ - keep:, :].T.reshape(2, HEAD_DIM, keep)
            if first:
                for later in range(layer + 1, DEPTH):
                    c_refs[gi][later] = jnp.zeros(c_refs[gi].shape[1:], F32)

    @pl.when(g == N_DIL - 1)
    def _():
        for p in range(PAIRS_PER_STEP):
            o_ref[:, p * LANES:(p + 1) * LANES] = (acc_ref[p] / l_ref[p]).astype(BF16)


def _dil_attn(h_a, tps, tcs, l, kv_rows):
    hv = h_a.reshape(BATCH, SEQ, A_COLS)
    steps = A_HEADS // 2 // PAIRS_PER_STEP
    heads = 2 * PAIRS_PER_STEP
    slab = (None, SEQ, LANES)
    col = lambda part, p: (lambda b, s, g: (b, 0, _group_step(g) * 12 + part * 4 + s * PAIRS_PER_STEP + p))
    tbl = pl.BlockSpec((None, heads, BLK, BLK), lambda b, s, g: (_group_step(g), s, 0, 0))
    first = kv_rows is None
    cache = lambda keep: pl.BlockSpec((DEPTH if first else None, None, 2, heads, HEAD_DIM, keep),
                                      lambda b, s, g: (0 if first else l, b, 0, s, 0, 0))
    in_specs = [pl.BlockSpec(slab, col(part, p)) for part in range(3) for p in range(PAIRS_PER_STEP)] + [tbl, tbl]
    operands = [hv] * (3 * PAIRS_PER_STEP) + [tps, tcs]
    aliases = {}
    if not first:
        aliases = {len(operands) + i: 1 + i for i in range(N_DIL)}
        in_specs += [pl.BlockSpec(memory_space=pl.ANY)] * N_DIL
        operands += list(kv_rows)
    outs = pl.pallas_call(
        functools.partial(_dil_attn_kernel, layer=l, first=first),
        grid=(BATCH, steps, N_DIL),
        in_specs=in_specs,
        out_specs=[pl.BlockSpec((None, SEQ, heads * HEAD_DIM), lambda b, s, g: (b, 0, s))]
        + [cache(keep) for keep in A_KEEP],
        out_shape=[jax.ShapeDtypeStruct((BATCH, SEQ, A_HD), BF16)]
        + [jax.ShapeDtypeStruct((DEPTH, BATCH, 2, A_HEADS, HEAD_DIM, keep), F32) for keep in A_KEEP],
        scratch_shapes=[pltpu.VMEM((PAIRS_PER_STEP, SEQ, LANES), F32)] * 3,
        input_output_aliases=aliases,
        compiler_params=_cparams("parallel", "parallel", "arbitrary"),
        name="dil_attn",
    )(*operands)
    return outs[0].reshape(BATCH * SEQ, A_HD), outs[1:]


def _band_b_kernel(sink_ref, q_ref, kvp_ref, kvc_ref, o_ref, c_ref):
    j = pl.program_id(1)
    lo = lax.broadcasted_iota(jnp.int32, (BLK, LANES), 1) < HEAD_DIM
    first = lax.broadcasted_iota(jnp.int32, (2 * BLK, 1), 0) < BLK

    def per_kv_head(x):
        rolled = pltpu.roll(x, HEAD_DIM, axis=1)
        return jnp.where(lo, x, rolled).astype(BF16), jnp.where(lo, rolled, x).astype(BF16)

    kp, kc = per_kv_head(kvp_ref[:, :B_KV]), per_kv_head(kvc_ref[:, :B_KV])
    vp, vc = per_kv_head(kvp_ref[:, B_KV:]), per_kv_head(kvc_ref[:, B_KV:])
    for pair in range(B_HEADS // 2):
        hk = (2 * pair) // B_GROUP
        sl = slice(pair * LANES, (pair + 1) * LANES)
        sink = jnp.where(first, sink_ref[2 * pair], sink_ref[2 * pair + 1])
        num, _, den = _pair_attend(_stack_pair(q_ref[:, sl]), (kc[hk], vc[hk], None), (kp[hk], vp[hk], None),
                                   j > 0, sink)
        o_ref[:, sl] = (num / den).astype(BF16)

    @pl.when(j == pl.num_programs(1) - 1)
    def _():
        c_ref[...] = kvc_ref[...].T


def _band_b(qb, kvb, sink):
    nb = SEQ // BLK
    qv = qb.reshape(BATCH, SEQ, B_Q)
    kvv = kvb.reshape(BATCH, SEQ, 2 * B_KV)
    prev = lambda j: jnp.maximum(j - 1, 0)
    o, c = pl.pallas_call(
        _band_b_kernel,
        grid=(BATCH, nb),
        in_specs=[
            pl.BlockSpec(memory_space=pltpu.SMEM),
            pl.BlockSpec((None, BLK, B_Q), lambda b, j: (b, j, 0)),
            pl.BlockSpec((None, BLK, 2 * B_KV), lambda b, j: (b, prev(j), 0)),
            pl.BlockSpec((None, BLK, 2 * B_KV), lambda b, j: (b, j, 0)),
        ],
        out_specs=[
            pl.BlockSpec((None, BLK, B_Q), lambda b, j: (b, j, 0)),
            pl.BlockSpec((None, 2 * B_KV, BLK), lambda b, j: (b, 0, 0)),
        ],
        out_shape=[
            jax.ShapeDtypeStruct((BATCH, SEQ, B_Q), BF16),
            jax.ShapeDtypeStruct((BATCH, 2 * B_KV, BLK), F32),
        ],
        compiler_params=_cparams("parallel", "arbitrary"),
        name="band_b",
    )(sink, qv, kvv, kvv)
    return o.reshape(BATCH * SEQ, B_Q), c


_N_HEADS = N_DIL * A_HEADS + B_HEADS
_COL_Q, _COL_K, _COL_V = 0, _N_HEADS, 2 * _N_HEADS
_COL_SINK_Q = N_DIL * A_HEADS


def _sample_attn_kernel(cols_ref, c0_ref, c1_ref, c2_ref, cb_ref, b0_ref, b1_ref, b2_ref, bnew_ref, sink_ref, o_ref):
    ns = SAMPLES_PER_STEP
    caches = (c0_ref, c1_ref, c2_ref)
    biases = (b0_ref, b1_ref, b2_ref)
    slab = cols_ref[...]
    k_new = pltpu.roll(slab, LANES - _COL_K, axis=2)
    v_new = pltpu.roll(slab, LANES - _COL_V, axis=2)
    s_new_row = jnp.sum(slab * k_new, axis=1, keepdims=True) + bnew_ref[...]
    lane_row = lax.broadcasted_iota(jnp.int32, (1, LANES), 1)
    lane = lax.broadcasted_iota(jnp.int32, (HEAD_DIM, LANES), 1)

    def per_head(x, start, count):
        return jnp.stack([jnp.stack([x[s, :, start + i:start + i + 1] for i in range(count)]) for s in range(ns)])

    def to_row(vals, start, rows):
        for s in range(ns):
            for i in range(vals.shape[1]):
                rows[s] = jnp.where(lane_row == start + i, vals[s, i], rows[s])

    def attend(start, count, kt_ref, vt_ref, bias_ref, sink):
        q = per_head(slab, start, count)
        s_new = per_head(s_new_row, start, count)
        tiles = []
        m_t = None
        for t in range(kt_ref.shape[-1] // LANES):
            sl = slice(t * LANES, (t + 1) * LANES)
            s = jnp.sum(kt_ref[:, :, :, sl] * q, axis=2, keepdims=True)
            if bias_ref is not None:
                bias = bias_ref[:, :, sl]
                s = jnp.where(bias > 0.5 * NEG_INF, s + bias, NEG_INF)
            tiles.append(s)
            m_t = s if m_t is None else jnp.maximum(m_t, s)
        m = jnp.maximum(jnp.max(m_t, axis=-1, keepdims=True), s_new)
        if sink is not None:
            m = jnp.maximum(m, sink)
        p_new = jnp.exp(s_new - m)
        den_t = jnp.zeros_like(m_t)
        pv_t = jnp.zeros((ns, count, HEAD_DIM, LANES), F32)
        for t, s in enumerate(tiles):
            p = jnp.exp(s - m)
            den_t = den_t + p
            pv_t = pv_t + vt_ref[:, :, :, t * LANES:(t + 1) * LANES] * p
        den = jnp.sum(den_t, axis=-1, keepdims=True) + p_new
        if sink is not None:
            den = den + jnp.exp(sink - m)
        return jnp.sum(pv_t, axis=-1, keepdims=True), m, p_new, den

    parts = [attend(g * A_HEADS, A_HEADS, caches[g].at[:, 0], caches[g].at[:, 1], biases[g], None)
             for g in range(N_DIL)]
    lses = [m + jnp.log(den) for _, m, _, den in parts]
    mx = jnp.maximum(jnp.maximum(lses[0], lses[1]), lses[2])
    es = [jnp.exp(lse - mx) for lse in lses]
    total = es[0] + es[1] + es[2]
    cache_a = jnp.zeros((ns, A_HEADS, HEAD_DIM, 1), F32)
    coef_rows = [jnp.zeros((1, LANES), F32) for _ in range(ns)]
    for g, (pv, _, p_new, den) in enumerate(parts):
        coef = es[g] / (total * den)
        cache_a = cache_a + coef * pv
        to_row(coef * p_new, g * A_HEADS, coef_rows)
    cache_b = []
    for hk in range(B_KV_HEADS):
        start = _COL_SINK_Q + hk * B_GROUP
        pv, _, p_new, den = attend(start, B_GROUP, cb_ref.at[:, 0, hk:hk + 1], cb_ref.at[:, 1, hk:hk + 1], None,
                                   sink_ref[hk * B_GROUP:(hk + 1) * B_GROUP])
        cache_b.append(pv / den)
        to_row(p_new / den, start, coef_rows)

    shift = lambda x, k: pltpu.roll(x, LANES - k, axis=1)
    for s in range(ns):
        out = jnp.zeros((HEAD_DIM, LANES), F32)
        for h in range(A_HEADS):
            out = jnp.where(lane == h, cache_a[s, h], out)
        for hq in range(B_HEADS):
            out = jnp.where(lane == A_HEADS + hq, cache_b[hq // B_GROUP][s, hq % B_GROUP], out)
        new = v_new[s] * coef_rows[s]
        groups = new + shift(new, A_HEADS) + shift(new, 2 * A_HEADS)
        o_ref[s] = out + jnp.where(lane < A_HEADS, groups, shift(new, 2 * A_HEADS))


def _sample_attn(h_a, qb, kvb, caches_t, l, bias_lanes, bias_new, sink):
    n = DEC_BATCH
    ha = h_a.reshape(n, N_DIL, 3, A_HEADS, HEAD_DIM)
    kb = jnp.repeat(kvb.reshape(n, 2, B_KV_HEADS, HEAD_DIM), B_GROUP, axis=2)
    flat = lambda part: ha[:, :, part].reshape(n, N_DIL * A_HEADS, HEAD_DIM)
    cols = jnp.concatenate([
        flat(0), qb.astype(F32).reshape(n, B_HEADS, HEAD_DIM), flat(1), kb[:, 0], flat(2), kb[:, 1],
        jnp.zeros((n, LANES - 3 * _N_HEADS, HEAD_DIM), F32)], axis=1)
    cols = cols.transpose(0, 2, 1)
    cache_spec = lambda c: pl.BlockSpec((None, SAMPLES_PER_STEP) + c.shape[2:], lambda b: (l, b, 0, 0, 0, 0))
    full = lambda a: pl.BlockSpec(a.shape, lambda b: (0,) * a.ndim)
    small = [a[:, None, :] for a in bias_lanes] + [bias_new, sink.reshape(B_HEADS, 1, 1)]
    out = pl.pallas_call(
        _sample_attn_kernel,
        grid=(n // SAMPLES_PER_STEP,),
        in_specs=[pl.BlockSpec((SAMPLES_PER_STEP, HEAD_DIM, LANES), lambda b: (b, 0, 0))]
        + [cache_spec(c) for c in caches_t] + [full(a) for a in small],
        out_specs=pl.BlockSpec((SAMPLES_PER_STEP, HEAD_DIM, LANES), lambda b: (b, 0, 0)),
        out_shape=jax.ShapeDtypeStruct((n, HEAD_DIM, LANES), F32),
        compiler_params=_cparams("parallel"),
        name="sample_attn",
    )(cols, *caches_t, *small)
    heads = out.transpose(0, 2, 1)
    out_a = heads[:, :A_HEADS].reshape(n, A_HD)
    out_b = heads[:, A_HEADS:A_HEADS + B_HEADS].reshape(n, B_Q)
    return out_a.astype(BF16), out_b.astype(BF16)


def _post_attn_kernel(a_ref, ob_ref, gate_ref, x_ref, mod_ref, wpa_ref, wpb_ref, wo_ref, lng_ref, lnb_ref, o_ref):
    ya = jnp.dot(a_ref[...], wpa_ref[...], preferred_element_type=F32)
    yb = jnp.dot(ob_ref[...], wpb_ref[...], preferred_element_type=F32)
    mix = gate_ref[:, :D_MODEL].astype(F32) * ya + gate_ref[:, D_MODEL:].astype(F32) * yb
    y = jnp.dot(mix.astype(BF16), wo_ref[...], preferred_element_type=F32)
    z = ALPHA * x_ref[...] + mod_ref[2] * y
    o_ref[...] = _layer_norm(z, lng_ref[0:1, :], lnb_ref[0:1, :])


def _post_attn(oa, ob, gates, x, mod, w_pa, w_pb, w_o, ln_g, ln_b, l, tm, rows_per_mod):
    t = x.shape[0]
    row = lambda w: pl.BlockSpec((tm, w), lambda i: (i, 0))
    const = lambda shape: pl.BlockSpec((None,) + shape, lambda i: (l, 0, 0), pipeline_mode=pl.Buffered(1))
    return pl.pallas_call(
        _post_attn_kernel,
        grid=(t // tm,),
        in_specs=[
            row(A_HD), row(B_Q), row(G_COLS), row(D_MODEL),
            _mod_spec(mod, tm, rows_per_mod),
            const((A_HD, D_MODEL)), const((B_Q, D_MODEL)), const((D_MODEL, D_MODEL)),
            const((2, D_MODEL)), const((2, D_MODEL)),
        ],
        out_specs=row(D_MODEL),
        out_shape=jax.ShapeDtypeStruct((t, D_MODEL), F32),
        compiler_params=_cparams("parallel"),
        name="post_attn",
    )(oa, ob, gates, x, mod, w_pa, w_pb, w_o, ln_g, ln_b)


def _ffn_kernel(x_ref, mod_ref, wg_ref, wu_ref, wd_ref, lng_ref, lnb_ref, o_ref, u_ref, acc_ref):
    k = pl.program_id(1)

    @pl.when(k == 0)
    def _():
        u_ref[...] = _modulate(x_ref, mod_ref, 3, 4)
        acc_ref[...] = jnp.zeros_like(acc_ref)

    u = u_ref[...]
    gate = jnp.dot(u, wg_ref[...], preferred_element_type=F32)
    up = jnp.dot(u, wu_ref[...], preferred_element_type=F32)
    act = (gate * jax.nn.sigmoid(gate) * up).astype(BF16)
    acc_ref[...] += jnp.dot(act, wd_ref[...], preferred_element_type=F32)

    @pl.when(k == pl.num_programs(1) - 1)
    def _():
        z = ALPHA * x_ref[...] + mod_ref[5] * acc_ref[...]
        o_ref[...] = _layer_norm(z, lng_ref[1:2, :], lnb_ref[1:2, :])


def _ffn(x, mod, w_gu, w_down, ln_g, ln_b, l, tm, rows_per_mod):
    t = x.shape[0]
    tf = 512
    nf = D_FF // tf
    return pl.pallas_call(
        _ffn_kernel,
        grid=(t // tm, nf),
        in_specs=[
            pl.BlockSpec((tm, D_MODEL), lambda i, k: (i, 0)),
            _mod_spec(mod, tm, rows_per_mod),
            pl.BlockSpec((None, D_MODEL, tf), lambda i, k: (l, 0, k)),
            pl.BlockSpec((None, D_MODEL, tf), lambda i, k: (l, 0, k + nf)),
            pl.BlockSpec((None, tf, D_MODEL), lambda i, k: (l, k, 0)),
            pl.BlockSpec((None, 2, D_MODEL), lambda i, k: (l, 0, 0)),
            pl.BlockSpec((None, 2, D_MODEL), lambda i, k: (l, 0, 0)),
        ],
        out_specs=pl.BlockSpec((tm, D_MODEL), lambda i, k: (i, 0)),
        out_shape=jax.ShapeDtypeStruct((t, D_MODEL), F32),
        scratch_shapes=[pltpu.VMEM((tm, D_MODEL), BF16), pltpu.VMEM((tm, D_MODEL), F32)],
        compiler_params=_cparams("parallel", "arbitrary"),
        name="ffn",
    )(x, mod, w_gu, w_gu, w_down, ln_g, ln_b)


def _t5_bucket(dist):
    exact = N_BUCKETS // 2
    n = jnp.maximum(dist, 0)
    log_ratio = jnp.log(jnp.maximum(n, exact).astype(F32) / exact) / math.log(T5_MAX_DIST / exact)
    large = jnp.minimum(exact + (log_ratio * (N_BUCKETS - exact)).astype(jnp.int32), N_BUCKETS - 1)
    return jnp.where(n < exact, n, large)


def _bias_tables(t5_table):
    tps, tcs, lanes, news = [], [], [], []
    for g, (w, d) in enumerate(DILATIONS):
        bucket = _t5_bucket(d * jnp.arange(BLK + 1))
        onehot = (bucket[:, None] == jnp.arange(N_BUCKETS)[None, :]).astype(F32)
        bias = jnp.dot(onehot, t5_table[:, g * A_HEADS:(g + 1) * A_HEADS],
                       precision=lax.Precision.HIGHEST).T
        wv = jnp.concatenate([bias[:, ::-1], jnp.zeros((A_HEADS, BLK), F32)], axis=1)
        toep = jnp.tile(wv, (1, BLK))[:, :BLK * 2 * BLK].reshape(A_HEADS, BLK, 2 * BLK)
        tps.append(toep[:, :, :BLK])
        tcs.append(toep[:, :, BLK:])
        by_row = jnp.repeat(bias[:, BLK:0:-1], d, axis=1)
        t = jnp.arange(BLK * d)[None, :]
        lanes.append(jnp.where(t % d == 0, by_row, NEG_INF))
        news.append(bias[:, 0])
    new = jnp.concatenate(news + [jnp.zeros((LANES - N_DIL * A_HEADS,), F32)])[None, :]
    return jnp.stack(tps), jnp.stack(tcs), lanes, new


def _rope_tables(pos):
    half = HEAD_DIM // 2
    inv = ROPE_THETA ** (-jnp.arange(half, dtype=F32) / half)
    ang = pos.astype(F32)[:, None] * inv[None]
    cos, sin = jnp.cos(ang), jnp.sin(ang)
    cos_t = jnp.concatenate([cos, cos, cos, cos], axis=-1)
    sin_t = jnp.concatenate([-sin, sin, -sin, sin], axis=-1)
    return cos_t, sin_t


def kernel(x_prompt, x_sample, c_prompt, c_sample, cache_a0, cache_a1, cache_a2, cache_b, t5_table, w_ada, b_ada,
           w_in, sinks, w_pa, w_pb, w_o, w_gu, w_down, ln_g, ln_b):
    tp_rows = BATCH * SEQ
    mod = _adaln(jnp.concatenate([c_prompt, c_sample], axis=0), w_ada, b_ada)
    mod_p = mod[:, :BATCH].reshape(DEPTH, BATCH, 6, 1, D_MODEL).transpose(0, 2, 1, 3, 4)
    mod_s = mod[:, BATCH:].reshape(DEPTH, 1, DEC_BATCH, 6, D_MODEL).transpose(0, 3, 1, 2, 4)

    q_scale = jnp.concatenate([jnp.full((A_HD,), Q_SCALE, F32), jnp.ones((2 * A_HD,), F32)])
    col_scale = jnp.concatenate([jnp.tile(q_scale, N_DIL), jnp.full((B_Q,), Q_SCALE, F32),
                                 jnp.ones((2 * B_KV + G_COLS,), F32)])
    w_a = w_b = w_g = (w_in * col_scale).astype(BF16)
    w_pa_h, w_pb_h, w_o_h = w_pa.astype(BF16), w_pb.astype(BF16), w_o.astype(BF16)
    w_gu_h, w_down_h = w_gu.astype(BF16), w_down.astype(BF16)

    tps, tcs, bias_lanes, bias_new = _bias_tables(t5_table)
    cos_p, sin_p = _rope_tables(jnp.arange(SEQ))
    cos_s, sin_s = _rope_tables(jnp.full((DEC_BATCH,), PAST_LEN))
    caches_t = [c.transpose(0, 1, 3, 4, 5, 2) for c in (cache_a0, cache_a1, cache_a2, cache_b)]

    xp = x_prompt.reshape(tp_rows, D_MODEL)
    xs = x_sample.reshape(DEC_BATCH, D_MODEL)
    kv_rows = None
    rows_a_s = [[] for _ in range(N_DIL)]
    rows_b_p, rows_b_s = [], []
    for l in range(DEPTH):
        sink = sinks[l].reshape(B_HEADS)

        h_a = _proj_a(xp, mod_p[l], w_a, l, 1024, SEQ)
        qb, kvb = _proj_b(xp, mod_p[l], w_b, l, cos_p, sin_p, 512, SEQ)
        gates = _gates(xp, mod_p[l], w_g, l, 1024, SEQ)
        oa, kv_rows = _dil_attn(h_a, tps, tcs, l, kv_rows)
        ob, b_rows = _band_b(qb, kvb, sink)
        xp = _post_attn(oa, ob, gates, xp, mod_p[l], w_pa_h, w_pb_h, w_o_h, ln_g, ln_b, l, 256, SEQ)
        xp = _ffn(xp, mod_p[l], w_gu_h, w_down_h, ln_g, ln_b, l, 512, SEQ)
        rows_b_p.append(b_rows.reshape(BATCH, 2, B_KV_HEADS, HEAD_DIM, BLK))

        h_a = _proj_a(xs, mod_s[l], w_a, l, DEC_BATCH, DEC_BATCH)
        qb, kvb = _proj_b(xs, mod_s[l], w_b, l, cos_s, sin_s, DEC_BATCH, DEC_BATCH)
        gates = _gates(xs, mod_s[l], w_g, l, DEC_BATCH, DEC_BATCH)
        oa, ob = _sample_attn(h_a, qb, kvb, caches_t, l, bias_lanes, bias_new, sink)
        xs = _post_attn(oa, ob, gates, xs, mod_s[l], w_pa_h, w_pb_h, w_o_h, ln_g, ln_b, l, DEC_BATCH, DEC_BATCH)
        xs = _ffn(xs, mod_s[l], w_gu_h, w_down_h, ln_g, ln_b, l, DEC_BATCH, DEC_BATCH)
        ha = h_a.reshape(DEC_BATCH, 1, N_DIL, 3, A_HEADS, HEAD_DIM)
        for g in range(N_DIL):
            rows_a_s[g].append(ha[:, :, g, 1:])
        rows_b_s.append(kvb.reshape(DEC_BATCH, 1, 2, B_KV_HEADS, HEAD_DIM))

    to_rows_major = lambda a: a.transpose(0, 1, 5, 2, 3, 4)
    outs = [xp.reshape(BATCH, SEQ, D_MODEL), xs.reshape(DEC_BATCH, 1, D_MODEL)]
    for g in range(N_DIL):
        outs.append(to_rows_major(kv_rows[g]))
        outs.append(jnp.stack(rows_a_s[g]))
    outs.append(to_rows_major(jnp.stack(rows_b_p)))
    outs.append(jnp.stack(rows_b_s))
    return tuple(outs)
```

```python
import functools
import math

import jax
import jax.numpy as jnp
from jax import lax
from jax.experimental import pallas as pl
from jax.experimental.pallas import tpu as pltpu

F32 = jnp.float32
BF16 = jnp.bfloat16

D_MODEL = 2048
BATCH = 8
SEQ = 2048
DEPTH = 2
DEC_BATCH = 128
PAST_LEN = 8192
HEAD_DIM = 64
DILATIONS = ((128, 1), (512, 4), (2048, 16))
N_DIL = 3
A_HEADS = 8
B_HEADS = 16
B_KV_HEADS = 2
B_GROUP = B_HEADS // B_KV_HEADS
B_WINDOW = 128
ROPE_THETA = 150000.0
N_BUCKETS = 32
T5_MAX_DIST = 2048
D_FF = ((8 * D_MODEL + 3 * 256 - 1) // (3 * 256)) * 256
ALPHA = (2 * DEPTH) ** 0.25
LN_EPS = 1e-5
NEG_INF = -1e30

BLK = 128
A_HD = A_HEADS * HEAD_DIM
A_GROUP_COLS = 3 * A_HD
A_COLS = N_DIL * A_GROUP_COLS
B_Q = B_HEADS * HEAD_DIM
B_KV = B_KV_HEADS * HEAD_DIM
B_COLS = B_Q + 2 * B_KV
G_COLS = 2 * D_MODEL
LANES = 128
PAIRS_PER_STEP = 2
ROW_TILE_PROJ = 1024
ROW_TILE_POST = 256
ROW_TILE_FFN = 512
B_BLOCKS_PER_STEP = 4
SAMPLES_PER_STEP = 2
UNIT_UNROLL = 4
Q_SCALE = HEAD_DIM ** -0.5
A_KEEP = tuple(min(w, SEQ) for w, _ in DILATIONS)

_VMEM_LIMIT = 56 * 1024 * 1024


def _cparams(*sem):
    return pltpu.CompilerParams(dimension_semantics=sem, vmem_limit_bytes=_VMEM_LIMIT)


def _mod_spec(mod, tm, rows_per_mod):
    mr = mod.shape[2]
    return pl.BlockSpec((6, None, mr, D_MODEL), lambda i, *_: (0, (i * tm) // rows_per_mod, 0, 0))


def _layer_norm(z, g, b):
    mu = jnp.mean(z, axis=-1, keepdims=True)
    zc = z - mu
    var = jnp.mean(zc * zc, axis=-1, keepdims=True)
    return zc * lax.rsqrt(var + LN_EPS) * g + b


def _adaln_kernel(c_ref, w_ref, b_ref, o_ref):
    c = c_ref[...]
    s = (c * jax.nn.sigmoid(c)).astype(BF16)
    o_ref[...] = jnp.dot(s, w_ref[...].astype(BF16), preferred_element_type=F32) + b_ref[...]


def _adaln(c_all, w_ada, b_ada):
    rows = c_all.shape[0]
    tn = 1024
    return pl.pallas_call(
        _adaln_kernel,
        grid=(DEPTH, 6 * D_MODEL // tn),
        in_specs=[
            pl.BlockSpec((rows, D_MODEL), lambda l, j: (0, 0)),
            pl.BlockSpec((None, D_MODEL, tn), lambda l, j: (l, 0, j)),
            pl.BlockSpec((None, 1, tn), lambda l, j: (l, 0, j)),
        ],
        out_specs=pl.BlockSpec((None, rows, tn), lambda l, j: (l, 0, j)),
        out_shape=jax.ShapeDtypeStruct((DEPTH, rows, 6 * D_MODEL), F32),
        compiler_params=_cparams("parallel", "parallel"),
        name="adaln",
    )(c_all, w_ada, b_ada.reshape(DEPTH, 1, 6 * D_MODEL))


def _modulate(x_ref, mod_ref, shift_idx, scale_idx):
    return (x_ref[...] * (1.0 + mod_ref[scale_idx]) + mod_ref[shift_idx]).astype(BF16)


def _proj_a_kernel(x_ref, mod_ref, w_ref, o_ref, u_ref):
    @pl.when(pl.program_id(1) == 0)
    def _():
        u_ref[...] = _modulate(x_ref, mod_ref, 0, 1)

    o_ref[...] = jnp.dot(u_ref[...], w_ref[...], preferred_element_type=F32)


def _proj_a(x, mod, w_a, l, tm, rows_per_mod):
    t = x.shape[0]
    return pl.pallas_call(
        _proj_a_kernel,
        grid=(t // tm, N_DIL),
        in_specs=[
            pl.BlockSpec((tm, D_MODEL), lambda i, g: (i, 0)),
            _mod_spec(mod, tm, rows_per_mod),
            pl.BlockSpec((None, D_MODEL, A_GROUP_COLS), lambda i, g: (l, 0, g)),
        ],
        out_specs=pl.BlockSpec((tm, A_GROUP_COLS), lambda i, g: (i, g)),
        out_shape=jax.ShapeDtypeStruct((t, A_COLS), F32),
        scratch_shapes=[pltpu.VMEM((tm, D_MODEL), BF16)],
        compiler_params=_cparams("parallel", "arbitrary"),
        name="proj_a",
    )(x, mod, w_a)


def _proj_b_kernel(x_ref, mod_ref, w_ref, cos_ref, sin_ref, q_ref, kv_ref):
    u = _modulate(x_ref, mod_ref, 0, 1)
    acc = jnp.dot(u, w_ref[0], preferred_element_type=F32)
    cos = cos_ref[...]
    sin = sin_ref[...]
    lane = lax.broadcasted_iota(jnp.int32, cos.shape, 1)
    first_half = (lane % HEAD_DIM) < (HEAD_DIM // 2)

    def rope(v):
        partner = jnp.where(first_half,
                            pltpu.roll(v, LANES - HEAD_DIM // 2, axis=1),
                            pltpu.roll(v, HEAD_DIM // 2, axis=1))
        return v * cos + partner * sin

    for c in range(B_Q // LANES):
        q_ref[:, c * LANES:(c + 1) * LANES] = rope(acc[:, c * LANES:(c + 1) * LANES]).astype(BF16)
    kv_ref[:, :B_KV] = rope(acc[:, B_Q:B_Q + B_KV])
    kv_ref[:, B_KV:] = acc[:, B_Q + B_KV:]


def _proj_b(x, mod, w_b, l, cos_t, sin_t, tm, rows_per_mod):
    t = x.shape[0]
    n_pos_blocks = cos_t.shape[0] // tm
    return pl.pallas_call(
        _proj_b_kernel,
        grid=(t // tm,),
        in_specs=[
            pl.BlockSpec((tm, D_MODEL), lambda i: (i, 0)),
            _mod_spec(mod, tm, rows_per_mod),
            pl.BlockSpec((pl.Element(1), pl.Element(D_MODEL), pl.Element(B_COLS)), lambda i: (l, 0, A_COLS)),
            pl.BlockSpec((tm, LANES), lambda i: (i % n_pos_blocks, 0)),
            pl.BlockSpec((tm, LANES), lambda i: (i % n_pos_blocks, 0)),
        ],
        out_specs=[
            pl.BlockSpec((tm, B_Q), lambda i: (i, 0)),
            pl.BlockSpec((tm, 2 * B_KV), lambda i: (i, 0)),
        ],
        out_shape=[
            jax.ShapeDtypeStruct((t, B_Q), BF16),
            jax.ShapeDtypeStruct((t, 2 * B_KV), F32),
        ],
        compiler_params=_cparams("parallel"),
        name="proj_b",
    )(x, mod, w_b, cos_t, sin_t)


def _gates_kernel(x_ref, mod_ref, w_ref, o_ref, u_ref):
    @pl.when(pl.program_id(1) == 0)
    def _():
        u_ref[...] = _modulate(x_ref, mod_ref, 0, 1)

    acc = jnp.dot(u_ref[...], w_ref[0], preferred_element_type=F32)
    o_ref[...] = jax.nn.sigmoid(acc).astype(o_ref.dtype)


def _gates(x, mod, w_g, l, tm, rows_per_mod):
    t = x.shape[0]
    tn = 1024
    return pl.pallas_call(
        _gates_kernel,
        grid=(t // tm, G_COLS // tn),
        in_specs=[
            pl.BlockSpec((tm, D_MODEL), lambda i, j: (i, 0)),
            _mod_spec(mod, tm, rows_per_mod),
            pl.BlockSpec((pl.Element(1), pl.Element(D_MODEL), pl.Element(tn)),
                         lambda i, j: (l, 0, pl.multiple_of(A_COLS + B_COLS + j * tn, LANES))),
        ],
        out_specs=pl.BlockSpec((tm, tn), lambda i, j: (i, j)),
        out_shape=jax.ShapeDtypeStruct((t, G_COLS), BF16),
        scratch_shapes=[pltpu.VMEM((tm, D_MODEL), BF16)],
        compiler_params=_cparams("parallel", "arbitrary"),
        name="gates",
    )(x, mod, w_g)


def _dot_nt(a, b):
    return lax.dot_general(a, b, (((1,), (1,)), ((), ())), preferred_element_type=F32)


def _band_masks():
    row = lax.broadcasted_iota(jnp.int32, (2 * BLK, BLK), 0) % BLK
    col = lax.broadcasted_iota(jnp.int32, (2 * BLK, BLK), 1)
    return col >= row, col <= row


def _rows(start, d):
    return pl.ds(start, BLK, stride=d) if d > 1 else pl.ds(pl.multiple_of(start, BLK), BLK)


def _stack_pair(q):
    lo = lax.broadcasted_iota(jnp.int32, (BLK, LANES), 1) < HEAD_DIM
    zero = jnp.zeros_like(q)
    return jnp.concatenate([jnp.where(lo, q, zero), jnp.where(lo, zero, q)], axis=0)


def _unstack_pair(x):
    lo = lax.broadcasted_iota(jnp.int32, (BLK, LANES), 1) < HEAD_DIM
    return jnp.where(lo, jnp.broadcast_to(x[:BLK], (BLK, LANES)), jnp.broadcast_to(x[BLK:], (BLK, LANES)))


def _pair_attend(qs, cur, prev, has_prev, sink):
    valid_p, valid_c = _band_masks()

    def scores(k, bias, valid):
        s = _dot_nt(qs, k)
        if bias is not None:
            s = s + bias
        return jnp.where(valid, s, NEG_INF)

    sc = scores(cur[0], cur[2], valid_c)
    if prev is not None:
        sp = jnp.where(has_prev, scores(prev[0], prev[2], valid_p), NEG_INF)
        m = jnp.max(jnp.maximum(sp, sc), axis=-1, keepdims=True)
    else:
        m = jnp.max(sc, axis=-1, keepdims=True)
    if sink is not None:
        m = jnp.maximum(m, sink)
    pc = jnp.exp(sc - m)
    pv = jnp.dot(pc.astype(BF16), cur[1], preferred_element_type=F32)
    if prev is not None:
        pp = jnp.exp(sp - m)
        pv = pv + jnp.dot(pp.astype(BF16), prev[1], preferred_element_type=F32)
        den = jnp.sum(pp + pc, axis=-1, keepdims=True)
    else:
        den = jnp.sum(pc, axis=-1, keepdims=True)
    if sink is not None:
        den = den + jnp.exp(sink - m)
    return _unstack_pair(pv), _unstack_pair(m), _unstack_pair(den)


def _dil_group(gi, q_refs, k_refs, v_refs, tp_ref, tc_ref, acc_ref, m_ref, l_ref):
    d = DILATIONS[gi][1]
    nblk = SEQ // (BLK * d)

    def unit(u, carry):
        r = u // nblk
        j = u % nblk
        start = j * (BLK * d) + r
        start_prev = jnp.maximum(j - 1, 0) * (BLK * d) + r
        rows = _rows(start, d)
        rows_prev = _rows(start_prev, d)
        for p in range(PAIRS_PER_STEP):
            qs = _stack_pair(q_refs[p][rows, :].astype(BF16))
            cur = (k_refs[p][rows, :].astype(BF16), v_refs[p][rows, :].astype(BF16),
                   tc_ref[2 * p:2 * p + 2].reshape(2 * BLK, BLK))
            prev = None
            if nblk > 1:
                prev = (k_refs[p][rows_prev, :].astype(BF16), v_refs[p][rows_prev, :].astype(BF16),
                        tp_ref[2 * p:2 * p + 2].reshape(2 * BLK, BLK))
            num, m_g, l_g = _pair_attend(qs, cur, prev, j > 0, None)
            if _group_step(gi) == 0:
                acc_ref[p, rows, :] = num
                m_ref[p, rows, :] = m_g
                l_ref[p, rows, :] = l_g
            else:
                m_old = m_ref[p, rows, :]
                m_new = jnp.maximum(m_old, m_g)
                a = jnp.exp(m_old - m_new)
                b = jnp.exp(m_g - m_new)
                acc_ref[p, rows, :] = acc_ref[p, rows, :] * a + num * b
                l_ref[p, rows, :] = l_ref[p, rows, :] * a + l_g * b
                m_ref[p, rows, :] = m_new
        return carry

    lax.fori_loop(0, SEQ // BLK, unit, 0, unroll=UNIT_UNROLL)


def _group_step(gi):
    return N_DIL - 1 - gi


def _dil_attn_kernel(q0, q1, k0, k1, v0, v1, tp_ref, tc_ref, *rest, layer, first):
    o_ref, c0_ref, c1_ref, c2_ref, acc_ref, m_ref, l_ref = rest[-7:]
    g = pl.program_id(2)
    q_refs, k_refs, v_refs = (q0, q1), (k0, k1), (v0, v1)
    c_refs = (c0_ref, c1_ref, c2_ref)
    for gi in range(N_DIL):
        @pl.when(g == _group_step(gi))
        def _(gi=gi):
            _dil_group(gi, q_refs, k_refs, v_refs, tp_ref, tc_ref, acc_ref, m_ref, l_ref)
            keep = A_KEEP[gi]
            c_ref = c_refs[gi].at[layer] if first else c_refs[gi]
            for p in range(PAIRS_PER_STEP):
                c_ref[0, 2 * p:2 * p + 2] = k_refs[p][SEQ - keep:, :].T.reshape(2, HEAD_DIM, keep)
                c_ref[1, 2 * p:2 * p + 2] = v_refs[p][SEQ - keep:, :].T.reshape(2, HEAD_DIM, keep)
            if first:
                for later in range(layer + 1, DEPTH):
                    c_refs[gi][later] = jnp.zeros(c_refs[gi].shape[1:], F32)

    @pl.when(g == N_DIL - 1)
    def _():
        for p in range(PAIRS_PER_STEP):
            o_ref[:, p * LANES:(p + 1) * LANES] = (acc_ref[p] / l_ref[p]).astype(BF16)


def _dil_attn(h_a, tps, tcs, l, kv_rows):
    hv = h_a.reshape(BATCH, SEQ, A_COLS)
    steps = A_HEADS // 2 // PAIRS_PER_STEP
    heads = 2 * PAIRS_PER_STEP
    slab = (None, SEQ, LANES)
    col = lambda part, p: (lambda b, s, g: (b, 0, _group_step(g) * 12 + part * 4 + s * PAIRS_PER_STEP + p))
    tbl = pl.BlockSpec((None, heads, BLK, BLK), lambda b, s, g: (_group_step(g), s, 0, 0))
    first = kv_rows is None
    cache = lambda keep: pl.BlockSpec((DEPTH if first else None, None, 2, heads, HEAD_DIM, keep),
                                      lambda b, s, g: (0 if first else l, b, 0, s, 0, 0))
    in_specs = [pl.BlockSpec(slab, col(part, p)) for part in range(3) for p in range(PAIRS_PER_STEP)] + [tbl, tbl]
    operands = [hv] * (3 * PAIRS_PER_STEP) + [tps, tcs]
    aliases = {}
    if not first:
        aliases = {len(operands) + i: 1 + i for i in range(N_DIL)}
        in_specs += [pl.BlockSpec(memory_space=pl.ANY)] * N_DIL
        operands += list(kv_rows)
    outs = pl.pallas_call(
        functools.partial(_dil_attn_kernel, layer=l, first=first),
        grid=(BATCH, steps, N_DIL),
        in_specs=in_specs,
        out_specs=[pl.BlockSpec((None, SEQ, heads * HEAD_DIM), lambda b, s, g: (b, 0, s))]
        + [cache(keep) for keep in A_KEEP],
        out_shape=[jax.ShapeDtypeStruct((BATCH, SEQ, A_HD), BF16)]
        + [jax.ShapeDtypeStruct((DEPTH, BATCH, 2, A_HEADS, HEAD_DIM, keep), F32) for keep in A_KEEP],
        scratch_shapes=[pltpu.VMEM((PAIRS_PER_STEP, SEQ, LANES), F32)] * 3,
        input_output_aliases=aliases,
        compiler_params=_cparams("parallel", "parallel", "arbitrary"),
        name="dil_attn",
    )(*operands)
    return outs[0].reshape(BATCH * SEQ, A_HD), outs[1:]


def _band_b_kernel(sink_ref, q_ref, kvp_ref, kvc_ref, o_ref, c_ref):
    j = pl.program_id(1)
    lo = lax.broadcasted_iota(jnp.int32, (BLK, LANES), 1) < HEAD_DIM
    first = lax.broadcasted_iota(jnp.int32, (2 * BLK, 1), 0) < BLK

    def per_kv_head(x):
        rolled = pltpu.roll(x, HEAD_DIM, axis=1)
        return jnp.where(lo, x, rolled).astype(BF16), jnp.where(lo, rolled, x).astype(BF16)

    k_blocks = [per_kv_head(kvp_ref[:, :B_KV])]
    v_blocks = [per_kv_head(kvp_ref[:, B_KV:])]
    for i in range(B_BLOCKS_PER_STEP):
        rows = slice(i * BLK, (i + 1) * BLK)
        k_blocks.append(per_kv_head(kvc_ref[rows, :B_KV]))
        v_blocks.append(per_kv_head(kvc_ref[rows, B_KV:]))
    for i in range(B_BLOCKS_PER_STEP):
        rows = slice(i * BLK, (i + 1) * BLK)
        has_prev = (j > 0) if i == 0 else True
        for pair in range(B_HEADS // 2):
            hk = (2 * pair) // B_GROUP
            sl = slice(pair * LANES, (pair + 1) * LANES)
            sink = jnp.where(first, sink_ref[2 * pair], sink_ref[2 * pair + 1])
            num, _, den = _pair_attend(_stack_pair(q_ref[rows, sl]), (k_blocks[i + 1][hk], v_blocks[i + 1][hk], None),
                                       (k_blocks[i][hk], v_blocks[i][hk], None), has_prev, sink)
            o_ref[rows, sl] = (num / den).astype(BF16)

    @pl.when(j == pl.num_programs(1) - 1)
    def _():
        c_ref[...] = kvc_ref[(B_BLOCKS_PER_STEP - 1) * BLK:, :].T


def _band_b(qb, kvb, sink):
    rows = B_BLOCKS_PER_STEP * BLK
    nb = SEQ // rows
    qv = qb.reshape(BATCH, SEQ, B_Q)
    kvv = kvb.reshape(BATCH, SEQ, 2 * B_KV)
    prev = lambda j: jnp.maximum(j * B_BLOCKS_PER_STEP - 1, 0)
    o, c = pl.pallas_call(
        _band_b_kernel,
        grid=(BATCH, nb),
        in_specs=[
            pl.BlockSpec(memory_space=pltpu.SMEM),
            pl.BlockSpec((None, rows, B_Q), lambda b, j: (b, j, 0)),
            pl.BlockSpec((None, BLK, 2 * B_KV), lambda b, j: (b, prev(j), 0)),
            pl.BlockSpec((None, rows, 2 * B_KV), lambda b, j: (b, j, 0)),
        ],
        out_specs=[
            pl.BlockSpec((None, rows, B_Q), lambda b, j: (b, j, 0)),
            pl.BlockSpec((None, 2 * B_KV, BLK), lambda b, j: (b, 0, 0)),
        ],
        out_shape=[
            jax.ShapeDtypeStruct((BATCH, SEQ, B_Q), BF16),
            jax.ShapeDtypeStruct((BATCH, 2 * B_KV, BLK), F32),
        ],
        compiler_params=_cparams("parallel", "arbitrary"),
        name="band_b",
    )(sink, qv, kvv, kvv)
    return o.reshape(BATCH * SEQ, B_Q), c


_N_HEADS = N_DIL * A_HEADS + B_HEADS
_COL_Q, _COL_K, _COL_V = 0, _N_HEADS, 2 * _N_HEADS
_COL_SINK_Q = N_DIL * A_HEADS


def _sample_attn_kernel(cols_ref, c0_ref, c1_ref, c2_ref, cb_ref, b0_ref, b1_ref, b2_ref, bnew_ref, sink_ref, o_ref):
    ns = SAMPLES_PER_STEP
    caches = (c0_ref, c1_ref, c2_ref)
    biases = (b0_ref, b1_ref, b2_ref)
    slab = cols_ref[...]
    k_new = pltpu.roll(slab, LANES - _COL_K, axis=2)
    v_new = pltpu.roll(slab, LANES - _COL_V, axis=2)
    s_new_row = jnp.sum(slab * k_new, axis=1, keepdims=True) + bnew_ref[...]
    lane_row = lax.broadcasted_iota(jnp.int32, (1, LANES), 1)
    lane = lax.broadcasted_iota(jnp.int32, (HEAD_DIM, LANES), 1)

    def per_head(x, start, count):
        return jnp.stack([jnp.stack([x[s, :, start + i:start + i + 1] for i in range(count)]) for s in range(ns)])

    def to_row(vals, start, rows):
        for s in range(ns):
            for i in range(vals.shape[1]):
                rows[s] = jnp.where(lane_row == start + i, vals[s, i], rows[s])

    def attend(start, count, kt_ref, vt_ref, bias_ref, sink):
        q = per_head(slab, start, count)
        s_new = per_head(s_new_row, start, count)
        tiles = []
        m_t = None
        for t in range(kt_ref.shape[-1] // LANES):
            sl = slice(t * LANES, (t + 1) * LANES)
            s = jnp.sum(kt_ref[:, :, :, sl] * q, axis=2, keepdims=True)
            if bias_ref is not None:
                bias = bias_ref[:, :, sl]
                s = jnp.where(bias > 0.5 * NEG_INF, s + bias, NEG_INF)
            tiles.append(s)
            m_t = s if m_t is None else jnp.maximum(m_t, s)
        m = jnp.maximum(jnp.max(m_t, axis=-1, keepdims=True), s_new)
        if sink is not None:
            m = jnp.maximum(m, sink)
        p_new = jnp.exp(s_new - m)
        den_t = jnp.zeros_like(m_t)
        pv_t = jnp.zeros((ns, count, HEAD_DIM, LANES), F32)
        for t, s in enumerate(tiles):
            p = jnp.exp(s - m)
            den_t = den_t + p
            pv_t = pv_t + vt_ref[:, :, :, t * LANES:(t + 1) * LANES] * p
        den = jnp.sum(den_t, axis=-1, keepdims=True) + p_new
        if sink is not None:
            den = den + jnp.exp(sink - m)
        return jnp.sum(pv_t, axis=-1, keepdims=True), m, p_new, den

    parts = [attend(g * A_HEADS, A_HEADS, caches[g].at[:, 0], caches[g].at[:, 1], biases[g], None)
             for g in range(N_DIL)]
    lses = [m + jnp.log(den) for _, m, _, den in parts]
    mx = jnp.maximum(jnp.maximum(lses[0], lses[1]), lses[2])
    es = [jnp.exp(lse - mx) for lse in lses]
    total = es[0] + es[1] + es[2]
    cache_a = jnp.zeros((ns, A_HEADS, HEAD_DIM, 1), F32)
    coef_rows = [jnp.zeros((1, LANES), F32) for _ in range(ns)]
    for g, (pv, _, p_new, den) in enumerate(parts):
        coef = es[g] / (total * den)
        cache_a = cache_a + coef * pv
        to_row(coef * p_new, g * A_HEADS, coef_rows)
    cache_b = []
    for hk in range(B_KV_HEADS):
        start = _COL_SINK_Q + hk * B_GROUP
        pv, _, p_new, den = attend(start, B_GROUP, cb_ref.at[:, 0, hk:hk + 1], cb_ref.at[:, 1, hk:hk + 1], None,
                                   sink_ref[hk * B_GROUP:(hk + 1) * B_GROUP])
        cache_b.append(pv / den)
        to_row(p_new / den, start, coef_rows)

    shift = lambda x, k: pltpu.roll(x, LANES - k, axis=1)
    for s in range(ns):
        out = jnp.zeros((HEAD_DIM, LANES), F32)
        for h in range(A_HEADS):
            out = jnp.where(lane == h, cache_a[s, h], out)
        for hq in range(B_HEADS):
            out = jnp.where(lane == A_HEADS + hq, cache_b[hq // B_GROUP][s, hq % B_GROUP], out)
        new = v_new[s] * coef_rows[s]
        groups = new + shift(new, A_HEADS) + shift(new, 2 * A_HEADS)
        o_ref[s] = out + jnp.where(lane < A_HEADS, groups, shift(new, 2 * A_HEADS))


def _sample_attn(h_a, qb, kvb, caches_t, l, bias_lanes, bias_new, sink):
    n = DEC_BATCH
    ha = h_a.reshape(n, N_DIL, 3, A_HEADS, HEAD_DIM)
    kb = jnp.repeat(kvb.reshape(n, 2, B_KV_HEADS, HEAD_DIM), B_GROUP, axis=2)
    flat = lambda part: ha[:, :, part].reshape(n, N_DIL * A_HEADS, HEAD_DIM)
    cols = jnp.concatenate([
        flat(0), qb.astype(F32).reshape(n, B_HEADS, HEAD_DIM), flat(1), kb[:, 0], flat(2), kb[:, 1],
        jnp.zeros((n, LANES - 3 * _N_HEADS, HEAD_DIM), F32)], axis=1)
    cols = cols.transpose(0, 2, 1)
    cache_spec = lambda c: pl.BlockSpec((None, SAMPLES_PER_STEP) + c.shape[2:], lambda b: (l, b, 0, 0, 0, 0))
    full = lambda a: pl.BlockSpec(a.shape, lambda b: (0,) * a.ndim)
    small = [a[:, None, :] for a in bias_lanes] + [bias_new, sink.reshape(B_HEADS, 1, 1)]
    out = pl.pallas_call(
        _sample_attn_kernel,
        grid=(n // SAMPLES_PER_STEP,),
        in_specs=[pl.BlockSpec((SAMPLES_PER_STEP, HEAD_DIM, LANES), lambda b: (b, 0, 0))]
        + [cache_spec(c) for c in caches_t] + [full(a) for a in small],
        out_specs=pl.BlockSpec((SAMPLES_PER_STEP, HEAD_DIM, LANES), lambda b: (b, 0, 0)),
        out_shape=jax.ShapeDtypeStruct((n, HEAD_DIM, LANES), F32),
        compiler_params=_cparams("parallel"),
        name="sample_attn",
    )(cols, *caches_t, *small)
    heads = out.transpose(0, 2, 1)
    out_a = heads[:, :A_HEADS].reshape(n, A_HD)
    out_b = heads[:, A_HEADS:A_HEADS + B_HEADS].reshape(n, B_Q)
    return out_a.astype(BF16), out_b.astype(BF16)


def _post_attn_kernel(a_ref, ob_ref, gate_ref, x_ref, mod_ref, wpa_ref, wpb_ref, wo_ref, lng_ref, lnb_ref, o_ref):
    ya = jnp.dot(a_ref[...], wpa_ref[...], preferred_element_type=F32)
    yb = jnp.dot(ob_ref[...], wpb_ref[...], preferred_element_type=F32)
    mix = gate_ref[:, :D_MODEL].astype(F32) * ya + gate_ref[:, D_MODEL:].astype(F32) * yb
    y = jnp.dot(mix.astype(BF16), wo_ref[...], preferred_element_type=F32)
    z = ALPHA * x_ref[...] + mod_ref[2] * y
    o_ref[...] = _layer_norm(z, lng_ref[0:1, :], lnb_ref[0:1, :])


def _post_attn(oa, ob, gates, x, mod, w_pa, w_pb, w_o, ln_g, ln_b, l, tm, rows_per_mod):
    t = x.shape[0]
    row = lambda w: pl.BlockSpec((tm, w), lambda i: (i, 0))
    const = lambda shape: pl.BlockSpec((None,) + shape, lambda i: (l, 0, 0), pipeline_mode=pl.Buffered(1))
    return pl.pallas_call(
        _post_attn_kernel,
        grid=(t // tm,),
        in_specs=[
            row(A_HD), row(B_Q), row(G_COLS), row(D_MODEL),
            _mod_spec(mod, tm, rows_per_mod),
            const((A_HD, D_MODEL)), const((B_Q, D_MODEL)), const((D_MODEL, D_MODEL)),
            const((2, D_MODEL)), const((2, D_MODEL)),
        ],
        out_specs=row(D_MODEL),
        out_shape=jax.ShapeDtypeStruct((t, D_MODEL), F32),
        compiler_params=_cparams("parallel"),
        name="post_attn",
    )(oa, ob, gates, x, mod, w_pa, w_pb, w_o, ln_g, ln_b)


def _ffn_kernel(x_ref, mod_ref, wg_ref, wu_ref, wd_ref, lng_ref, lnb_ref, o_ref, u_ref, acc_ref):
    k = pl.program_id(1)

    @pl.when(k == 0)
    def _():
        u_ref[...] = _modulate(x_ref, mod_ref, 3, 4)
        acc_ref[...] = jnp.zeros_like(acc_ref)

    u = u_ref[...]
    gate = jnp.dot(u, wg_ref[...], preferred_element_type=F32)
    up = jnp.dot(u, wu_ref[...], preferred_element_type=F32)
    act = (gate * jax.nn.sigmoid(gate) * up).astype(BF16)
    acc_ref[...] += jnp.dot(act, wd_ref[...], preferred_element_type=F32)

    @pl.when(k == pl.num_programs(1) - 1)
    def _():
        z = ALPHA * x_ref[...] + mod_ref[5] * acc_ref[...]
        o_ref[...] = _layer_norm(z, lng_ref[1:2, :], lnb_ref[1:2, :])


def _ffn(x, mod, w_gu, w_down, ln_g, ln_b, l, tm, rows_per_mod):
    t = x.shape[0]
    tf = 512
    nf = D_FF // tf
    return pl.pallas_call(
        _ffn_kernel,
        grid=(t // tm, nf),
        in_specs=[
            pl.BlockSpec((tm, D_MODEL), lambda i, k: (i, 0)),
            _mod_spec(mod, tm, rows_per_mod),
            pl.BlockSpec((None, D_MODEL, tf), lambda i, k: (l, 0, k)),
            pl.BlockSpec((None, D_MODEL, tf), lambda i, k: (l, 0, k + nf)),
            pl.BlockSpec((None, tf, D_MODEL), lambda i, k: (l, k, 0)),
            pl.BlockSpec((None, 2, D_MODEL), lambda i, k: (l, 0, 0)),
            pl.BlockSpec((None, 2, D_MODEL), lambda i, k: (l, 0, 0)),
        ],
        out_specs=pl.BlockSpec((tm, D_MODEL), lambda i, k: (i, 0)),
        out_shape=jax.ShapeDtypeStruct((t, D_MODEL), F32),
        scratch_shapes=[pltpu.VMEM((tm, D_MODEL), BF16), pltpu.VMEM((tm, D_MODEL), F32)],
        compiler_params=_cparams("parallel", "arbitrary"),
        name="ffn",
    )(x, mod, w_gu, w_gu, w_down, ln_g, ln_b)


def _t5_bucket(dist):
    exact = N_BUCKETS // 2
    n = jnp.maximum(dist, 0)
    log_ratio = jnp.log(jnp.maximum(n, exact).astype(F32) / exact) / math.log(T5_MAX_DIST / exact)
    large = jnp.minimum(exact + (log_ratio * (N_BUCKETS - exact)).astype(jnp.int32), N_BUCKETS - 1)
    return jnp.where(n < exact, n, large)


def _bias_tables(t5_table):
    tps, tcs, lanes, news = [], [], [], []
    for g, (w, d) in enumerate(DILATIONS):
        bucket = _t5_bucket(d * jnp.arange(BLK + 1))
        onehot = (bucket[:, None] == jnp.arange(N_BUCKETS)[None, :]).astype(F32)
        bias = jnp.dot(onehot, t5_table[:, g * A_HEADS:(g + 1) * A_HEADS],
                       precision=lax.Precision.HIGHEST).T
        wv = jnp.concatenate([bias[:, ::-1], jnp.zeros((A_HEADS, BLK), F32)], axis=1)
        toep = jnp.tile(wv, (1, BLK))[:, :BLK * 2 * BLK].reshape(A_HEADS, BLK, 2 * BLK)
        tps.append(toep[:, :, :BLK])
        tcs.append(toep[:, :, BLK:])
        by_row = jnp.repeat(bias[:, BLK:0:-1], d, axis=1)
        t = jnp.arange(BLK * d)[None, :]
        lanes.append(jnp.where(t % d == 0, by_row, NEG_INF))
        news.append(bias[:, 0])
    new = jnp.concatenate(news + [jnp.zeros((LANES - N_DIL * A_HEADS,), F32)])[None, :]
    return jnp.stack(tps), jnp.stack(tcs), lanes, new


def _rope_tables(pos):
    half = HEAD_DIM // 2
    inv = ROPE_THETA ** (-jnp.arange(half, dtype=F32) / half)
    ang = pos.astype(F32)[:, None] * inv[None]
    cos, sin = jnp.cos(ang), jnp.sin(ang)
    cos_t = jnp.concatenate([cos, cos, cos, cos], axis=-1)
    sin_t = jnp.concatenate([-sin, sin, -sin, sin], axis=-1)
    return cos_t, sin_t


def kernel(x_prompt, x_sample, c_prompt, c_sample, cache_a0, cache_a1, cache_a2, cache_b, t5_table, w_ada, b_ada,
           w_in, sinks, w_pa, w_pb, w_o, w_gu, w_down, ln_g, ln_b):
    tp_rows = BATCH * SEQ
    mod = _adaln(jnp.concatenate([c_prompt, c_sample], axis=0), w_ada, b_ada)
    mod_p = mod[:, :BATCH].reshape(DEPTH, BATCH, 6, 1, D_MODEL).transpose(0, 2, 1, 3, 4)
    mod_s = mod[:, BATCH:].reshape(DEPTH, 1, DEC_BATCH, 6, D_MODEL).transpose(0, 3, 1, 2, 4)

    q_scale = jnp.concatenate([jnp.full((A_HD,), Q_SCALE, F32), jnp.ones((2 * A_HD,), F32)])
    col_scale = jnp.concatenate([jnp.tile(q_scale, N_DIL), jnp.full((B_Q,), Q_SCALE, F32),
                                 jnp.ones((2 * B_KV + G_COLS,), F32)])
    w_a = w_b = w_g = (w_in * col_scale).astype(BF16)
    w_pa_h, w_pb_h, w_o_h = w_pa.astype(BF16), w_pb.astype(BF16), w_o.astype(BF16)
    w_gu_h, w_down_h = w_gu.astype(BF16), w_down.astype(BF16)

    tps, tcs, bias_lanes, bias_new = _bias_tables(t5_table)
    cos_p, sin_p = _rope_tables(jnp.arange(SEQ))
    cos_s, sin_s = _rope_tables(jnp.full((DEC_BATCH,), PAST_LEN))
    caches_t = [c.transpose(0, 1, 3, 4, 5, 2) for c in (cache_a0, cache_a1, cache_a2, cache_b)]

    xp = x_prompt.reshape(tp_rows, D_MODEL)
    xs = x_sample.reshape(DEC_BATCH, D_MODEL)
    kv_rows = None
    rows_a_s = [[] for _ in range(N_DIL)]
    rows_b_p, rows_b_s = [], []
    for l in range(DEPTH):
        sink = sinks[l].reshape(B_HEADS)

        h_a = _proj_a(xp, mod_p[l], w_a, l, ROW_TILE_PROJ, SEQ)
        qb, kvb = _proj_b(xp, mod_p[l], w_b, l, cos_p, sin_p, ROW_TILE_PROJ, SEQ)
        gates = _gates(xp, mod_p[l], w_g, l, ROW_TILE_PROJ, SEQ)
        oa, kv_rows = _dil_attn(h_a, tps, tcs, l, kv_rows)
        ob, b_rows = _band_b(qb, kvb, sink)
        xp = _post_attn(oa, ob, gates, xp, mod_p[l], w_pa_h, w_pb_h, w_o_h, ln_g, ln_b, l, ROW_TILE_POST, SEQ)
        xp = _ffn(xp, mod_p[l], w_gu_h, w_down_h, ln_g, ln_b, l, ROW_TILE_FFN, SEQ)
        rows_b_p.append(b_rows.reshape(BATCH, 2, B_KV_HEADS, HEAD_DIM, BLK))

        h_a = _proj_a(xs, mod_s[l], w_a, l, DEC_BATCH, DEC_BATCH)
        qb, kvb = _proj_b(xs, mod_s[l], w_b, l, cos_s, sin_s, DEC_BATCH, DEC_BATCH)
        gates = _gates(xs, mod_s[l], w_g, l, DEC_BATCH, DEC_BATCH)
        oa, ob = _sample_attn(h_a, qb, kvb, caches_t, l, bias_lanes, bias_new, sink)
        xs = _post_attn(oa, ob, gates, xs, mod_s[l], w_pa_h, w_pb_h, w_o_h, ln_g, ln_b, l, DEC_BATCH, DEC_BATCH)
        xs = _ffn(xs, mod_s[l], w_gu_h, w_down_h, ln_g, ln_b, l, DEC_BATCH, DEC_BATCH)
        ha = h_a.reshape(DEC_BATCH, 1, N_DIL, 3, A_HEADS, HEAD_DIM)
        for g in range(N_DIL):
            rows_a_s[g].append(ha[:, :, g, 1:])
        rows_b_s.append(kvb.reshape(DEC_BATCH, 1, 2, B_KV_HEADS, HEAD_DIM))

    to_rows_major = lambda a: a.transpose(0, 1, 5, 2, 3, 4)
    outs = [xp.reshape(BATCH, SEQ, D_MODEL), xs.reshape(DEC_BATCH, 1, D_MODEL)]
    for g in range(N_DIL):
        outs.append(to_rows_major(kv_rows[g]))
        outs.append(jnp.stack(rows_a_s[g]))
    outs.append(to_rows_major(jnp.stack(rows_b_p)))
    outs.append(jnp.stack(rows_b_s))
    return tuple(outs)
```

```python
import functools
import math

import jax
import jax.numpy as jnp
from jax import lax
from jax.experimental import pallas as pl
from jax.experimental.pallas import tpu as pltpu

F32 = jnp.float32
BF16 = jnp.bfloat16

D_MODEL = 2048
BATCH = 8
SEQ = 2048
DEPTH = 2
DEC_BATCH = 128
PAST_LEN = 8192
HEAD_DIM = 64
DILATIONS = ((128, 1), (512, 4), (2048, 16))
N_DIL = 3
A_HEADS = 8
B_HEADS = 16
B_KV_HEADS = 2
B_GROUP = B_HEADS // B_KV_HEADS
B_WINDOW = 128
ROPE_THETA = 150000.0
N_BUCKETS = 32
T5_MAX_DIST = 2048
D_FF = ((8 * D_MODEL + 3 * 256 - 1) // (3 * 256)) * 256
ALPHA = (2 * DEPTH) ** 0.25
LN_EPS = 1e-5
NEG_INF = -1e30

BLK = 128
A_HD = A_HEADS * HEAD_DIM
A_GROUP_COLS = 3 * A_HD
A_COLS = N_DIL * A_GROUP_COLS
B_Q = B_HEADS * HEAD_DIM
B_KV = B_KV_HEADS * HEAD_DIM
B_COLS = B_Q + 2 * B_KV
G_COLS = 2 * D_MODEL
LANES = 128
PAIRS_PER_STEP = 2
ROW_TILE_PROJ = 1024
ROW_TILE_POST = 256
ROW_TILE_FFN = 512
B_BLOCKS_PER_STEP = 4
SAMPLES_PER_STEP = 2
UNIT_UNROLL = 4
Q_SCALE = HEAD_DIM ** -0.5
A_KEEP = tuple(min(w, SEQ) for w, _ in DILATIONS)

V7X_VMEM_BYTES = 64 * 1024 * 1024
_VMEM_LIMIT = V7X_VMEM_BYTES * 7 // 8


def _cparams(*sem):
    return pltpu.CompilerParams(dimension_semantics=sem, vmem_limit_bytes=_VMEM_LIMIT)


def _mod_spec(mod, tm, rows_per_mod):
    mr = mod.shape[2]
    return pl.BlockSpec((6, None, mr, D_MODEL), lambda i, *_: (0, (i * tm) // rows_per_mod, 0, 0))


def _layer_norm(z, g, b):
    mu = jnp.mean(z, axis=-1, keepdims=True)
    zc = z - mu
    var = jnp.mean(zc * zc, axis=-1, keepdims=True)
    return zc * lax.rsqrt(var + LN_EPS) * g + b


def _adaln_kernel(c_ref, w_ref, b_ref, o_ref):
    c = c_ref[...]
    s = (c * jax.nn.sigmoid(c)).astype(BF16)
    o_ref[...] = jnp.dot(s, w_ref[...].astype(BF16), preferred_element_type=F32) + b_ref[...]


def _adaln(c_all, w_ada, b_ada):
    rows = c_all.shape[0]
    tn = 1024
    return pl.pallas_call(
        _adaln_kernel,
        grid=(DEPTH, 6 * D_MODEL // tn),
        in_specs=[
            pl.BlockSpec((rows, D_MODEL), lambda l, j: (0, 0)),
            pl.BlockSpec((None, D_MODEL, tn), lambda l, j: (l, 0, j)),
            pl.BlockSpec((None, 1, tn), lambda l, j: (l, 0, j)),
        ],
        out_specs=pl.BlockSpec((None, rows, tn), lambda l, j: (l, 0, j)),
        out_shape=jax.ShapeDtypeStruct((DEPTH, rows, 6 * D_MODEL), F32),
        compiler_params=_cparams("parallel", "parallel"),
        name="adaln",
    )(c_all, w_ada, b_ada.reshape(DEPTH, 1, 6 * D_MODEL))


def _modulate(x_ref, mod_ref, shift_idx, scale_idx):
    return (x_ref[...] * (1.0 + mod_ref[scale_idx]) + mod_ref[shift_idx]).astype(BF16)


def _proj_a_kernel(x_ref, mod_ref, w_ref, o_ref, u_ref):
    @pl.when(pl.program_id(1) == 0)
    def _():
        u_ref[...] = _modulate(x_ref, mod_ref, 0, 1)

    o_ref[...] = jnp.dot(u_ref[...], w_ref[...], preferred_element_type=F32)


def _proj_a(x, mod, w_a, l, tm, rows_per_mod):
    t = x.shape[0]
    return pl.pallas_call(
        _proj_a_kernel,
        grid=(t // tm, N_DIL),
        in_specs=[
            pl.BlockSpec((tm, D_MODEL), lambda i, g: (i, 0)),
            _mod_spec(mod, tm, rows_per_mod),
            pl.BlockSpec((None, D_MODEL, A_GROUP_COLS), lambda i, g: (l, 0, g)),
        ],
        out_specs=pl.BlockSpec((tm, A_GROUP_COLS), lambda i, g: (i, g)),
        out_shape=jax.ShapeDtypeStruct((t, A_COLS), F32),
        scratch_shapes=[pltpu.VMEM((tm, D_MODEL), BF16)],
        compiler_params=_cparams("parallel", "arbitrary"),
        name="proj_a",
    )(x, mod, w_a)


def _proj_b_kernel(x_ref, mod_ref, w_ref, cos_ref, sin_ref, q_ref, kv_ref):
    u = _modulate(x_ref, mod_ref, 0, 1)
    acc = jnp.dot(u, w_ref[0], preferred_element_type=F32)
    cos = cos_ref[...]
    sin = sin_ref[...]
    lane = lax.broadcasted_iota(jnp.int32, cos.shape, 1)
    first_half = (lane % HEAD_DIM) < (HEAD_DIM // 2)

    def rope(v):
        partner = jnp.where(first_half,
                            pltpu.roll(v, LANES - HEAD_DIM // 2, axis=1),
                            pltpu.roll(v, HEAD_DIM // 2, axis=1))
        return v * cos + partner * sin

    for c in range(B_Q // LANES):
        q_ref[:, c * LANES:(c + 1) * LANES] = rope(acc[:, c * LANES:(c + 1) * LANES]).astype(BF16)
    kv_ref[:, :B_KV] = rope(acc[:, B_Q:B_Q + B_KV])
    kv_ref[:, B_KV:] = acc[:, B_Q + B_KV:]


def _proj_b(x, mod, w_b, l, cos_t, sin_t, tm, rows_per_mod):
    t = x.shape[0]
    n_pos_blocks = cos_t.shape[0] // tm
    return pl.pallas_call(
        _proj_b_kernel,
        grid=(t // tm,),
        in_specs=[
            pl.BlockSpec((tm, D_MODEL), lambda i: (i, 0)),
            _mod_spec(mod, tm, rows_per_mod),
            pl.BlockSpec((pl.Element(1), pl.Element(D_MODEL), pl.Element(B_COLS)), lambda i: (l, 0, A_COLS)),
            pl.BlockSpec((tm, LANES), lambda i: (i % n_pos_blocks, 0)),
            pl.BlockSpec((tm, LANES), lambda i: (i % n_pos_blocks, 0)),
        ],
        out_specs=[
            pl.BlockSpec((tm, B_Q), lambda i: (i, 0)),
            pl.BlockSpec((tm, 2 * B_KV), lambda i: (i, 0)),
        ],
        out_shape=[
            jax.ShapeDtypeStruct((t, B_Q), BF16),
            jax.ShapeDtypeStruct((t, 2 * B_KV), F32),
        ],
        compiler_params=_cparams("parallel"),
        name="proj_b",
    )(x, mod, w_b, cos_t, sin_t)


def _gates_kernel(x_ref, mod_ref, w_ref, o_ref, u_ref):
    @pl.when(pl.program_id(1) == 0)
    def _():
        u_ref[...] = _modulate(x_ref, mod_ref, 0, 1)

    acc = jnp.dot(u_ref[...], w_ref[0], preferred_element_type=F32)
    o_ref[...] = jax.nn.sigmoid(acc).astype(o_ref.dtype)


def _gates(x, mod, w_g, l, tm, rows_per_mod):
    t = x.shape[0]
    tn = 2048
    return pl.pallas_call(
        _gates_kernel,
        grid=(t // tm, G_COLS // tn),
        in_specs=[
            pl.BlockSpec((tm, D_MODEL), lambda i, j: (i, 0)),
            _mod_spec(mod, tm, rows_per_mod),
            pl.BlockSpec((pl.Element(1), pl.Element(D_MODEL), pl.Element(tn)),
                         lambda i, j: (l, 0, pl.multiple_of(A_COLS + B_COLS + j * tn, LANES))),
        ],
        out_specs=pl.BlockSpec((tm, tn), lambda i, j: (i, j)),
        out_shape=jax.ShapeDtypeStruct((t, G_COLS), BF16),
        scratch_shapes=[pltpu.VMEM((tm, D_MODEL), BF16)],
        compiler_params=_cparams("parallel", "arbitrary"),
        name="gates",
    )(x, mod, w_g)


def _dot_nt(a, b):
    return lax.dot_general(a, b, (((1,), (1,)), ((), ())), preferred_element_type=F32)


def _band_masks():
    row = lax.broadcasted_iota(jnp.int32, (2 * BLK, BLK), 0) % BLK
    col = lax.broadcasted_iota(jnp.int32, (2 * BLK, BLK), 1)
    return col >= row, col <= row


def _rows(start, d):
    return pl.ds(start, BLK, stride=d) if d > 1 else pl.ds(pl.multiple_of(start, BLK), BLK)


def _stack_pair(q):
    lo = lax.broadcasted_iota(jnp.int32, (BLK, LANES), 1) < HEAD_DIM
    zero = jnp.zeros_like(q)
    return jnp.concatenate([jnp.where(lo, q, zero), jnp.where(lo, zero, q)], axis=0)


def _unstack_pair(x):
    lo = lax.broadcasted_iota(jnp.int32, (BLK, LANES), 1) < HEAD_DIM
    return jnp.where(lo, jnp.broadcast_to(x[:BLK], (BLK, LANES)), jnp.broadcast_to(x[BLK:], (BLK, LANES)))


def _pair_attend(qs, cur, prev, has_prev, sink):
    valid_p, valid_c = _band_masks()

    def scores(k, bias, valid):
        s = _dot_nt(qs, k)
        if bias is not None:
            s = s + bias
        return jnp.where(valid, s, NEG_INF)

    sc = scores(cur[0], cur[2], valid_c)
    if prev is not None:
        sp = jnp.where(has_prev, scores(prev[0], prev[2], valid_p), NEG_INF)
        m = jnp.max(jnp.maximum(sp, sc), axis=-1, keepdims=True)
    else:
        m = jnp.max(sc, axis=-1, keepdims=True)
    if sink is not None:
        m = jnp.maximum(m, sink)
    pc = jnp.exp(sc - m)
    pv = jnp.dot(pc.astype(BF16), cur[1], preferred_element_type=F32)
    if prev is not None:
        pp = jnp.exp(sp - m)
        pv = pv + jnp.dot(pp.astype(BF16), prev[1], preferred_element_type=F32)
        den = jnp.sum(pp + pc, axis=-1, keepdims=True)
    else:
        den = jnp.sum(pc, axis=-1, keepdims=True)
    if sink is not None:
        den = den + jnp.exp(sink - m)
    return _unstack_pair(pv), _unstack_pair(m), _unstack_pair(den)


def _dil_group(gi, q_refs, k_refs, v_refs, tp_ref, tc_ref, acc_ref, m_ref, l_ref):
    d = DILATIONS[gi][1]
    nblk = SEQ // (BLK * d)

    def unit(u, carry):
        r = u // nblk
        j = u % nblk
        start = j * (BLK * d) + r
        start_prev = jnp.maximum(j - 1, 0) * (BLK * d) + r
        rows = _rows(start, d)
        rows_prev = _rows(start_prev, d)
        for p in range(PAIRS_PER_STEP):
            qs = _stack_pair(q_refs[p][rows, :].astype(BF16))
            cur = (k_refs[p][rows, :].astype(BF16), v_refs[p][rows, :].astype(BF16),
                   tc_ref[2 * p:2 * p + 2].reshape(2 * BLK, BLK))
            prev = None
            if nblk > 1:
                prev = (k_refs[p][rows_prev, :].astype(BF16), v_refs[p][rows_prev, :].astype(BF16),
                        tp_ref[2 * p:2 * p + 2].reshape(2 * BLK, BLK))
            num, m_g, l_g = _pair_attend(qs, cur, prev, j > 0, None)
            if _group_step(gi) == 0:
                acc_ref[p, rows, :] = num
                m_ref[p, rows, :] = m_g
                l_ref[p, rows, :] = l_g
            else:
                m_old = m_ref[p, rows, :]
                m_new = jnp.maximum(m_old, m_g)
                a = jnp.exp(m_old - m_new)
                b = jnp.exp(m_g - m_new)
                acc_ref[p, rows, :] = acc_ref[p, rows, :] * a + num * b
                l_ref[p, rows, :] = l_ref[p, rows, :] * a + l_g * b
                m_ref[p, rows, :] = m_new
        return carry

    lax.fori_loop(0, SEQ // BLK, unit, 0, unroll=UNIT_UNROLL)


def _group_step(gi):
    return N_DIL - 1 - gi


def _dil_attn_kernel(q0, q1, k0, k1, v0, v1, tp_ref, tc_ref, *rest, layer, first):
    o_ref, c0_ref, c1_ref, c2_ref, acc_ref, m_ref, l_ref = rest[-7:]
    g = pl.program_id(2)
    q_refs, k_refs, v_refs = (q0, q1), (k0, k1), (v0, v1)
    c_refs = (c0_ref, c1_ref, c2_ref)
    for gi in range(N_DIL):
        @pl.when(g == _group_step(gi))
        def _(gi=gi):
            _dil_group(gi, q_refs, k_refs, v_refs, tp_ref, tc_ref, acc_ref, m_ref, l_ref)
            keep = A_KEEP[gi]
            c_ref = c_refs[gi].at[layer] if first else c_refs[gi]
            for p in range(PAIRS_PER_STEP):
                c_ref[0, 2 * p:2 * p + 2] = k_refs[p][SEQ - keep:, :].T.reshape(2, HEAD_DIM, keep)
                c_ref[1, 2 * p:2 * p + 2] = v_refs[p][SEQ - keep:, :].T.reshape(2, HEAD_DIM, keep)
            if first:
                for later in range(layer + 1, DEPTH):
                    c_refs[gi][later] = jnp.zeros(c_refs[gi].shape[1:], F32)

    @pl.when(g == N_DIL - 1)
    def _():
        for p in range(PAIRS_PER_STEP):
            o_ref[:, p * LANES:(p + 1) * LANES] = (acc_ref[p] / l_ref[p]).astype(BF16)


def _dil_attn(h_a, tps, tcs, l, kv_rows):
    hv = h_a.reshape(BATCH, SEQ, A_COLS)
    steps = A_HEADS // 2 // PAIRS_PER_STEP
    heads = 2 * PAIRS_PER_STEP
    slab = (None, SEQ, LANES)
    col = lambda part, p: (lambda b, s, g: (b, 0, _group_step(g) * 12 + part * 4 + s * PAIRS_PER_STEP + p))
    tbl = pl.BlockSpec((None, heads, BLK, BLK), lambda b, s, g: (_group_step(g), s, 0, 0))
    first = kv_rows is None
    cache = lambda keep: pl.BlockSpec((DEPTH if first else None, None, 2, heads, HEAD_DIM, keep),
                                      lambda b, s, g: (0 if first else l, b, 0, s, 0, 0))
    in_specs = [pl.BlockSpec(slab, col(part, p)) for part in range(3) for p in range(PAIRS_PER_STEP)] + [tbl, tbl]
    operands = [hv] * (3 * PAIRS_PER_STEP) + [tps, tcs]
    aliases = {}
    if not first:
        aliases = {len(operands) + i: 1 + i for i in range(N_DIL)}
        in_specs += [pl.BlockSpec(memory_space=pl.ANY)] * N_DIL
        operands += list(kv_rows)
    outs = pl.pallas_call(
        functools.partial(_dil_attn_kernel, layer=l, first=first),
        grid=(BATCH, steps, N_DIL),
        in_specs=in_specs,
        out_specs=[pl.BlockSpec((None, SEQ, heads * HEAD_DIM), lambda b, s, g: (b, 0, s))]
        + [cache(keep) for keep in A_KEEP],
        out_shape=[jax.ShapeDtypeStruct((BATCH, SEQ, A_HD), BF16)]
        + [jax.ShapeDtypeStruct((DEPTH, BATCH, 2, A_HEADS, HEAD_DIM, keep), F32) for keep in A_KEEP],
        scratch_shapes=[pltpu.VMEM((PAIRS_PER_STEP, SEQ, LANES), F32)] * 3,
        input_output_aliases=aliases,
        compiler_params=_cparams("parallel", "parallel", "arbitrary"),
        name="dil_attn",
    )(*operands)
    return outs[0].reshape(BATCH * SEQ, A_HD), outs[1:]


def _band_b_kernel(sink_ref, q_ref, kvp_ref, kvc_ref, o_ref, c_ref):
    j = pl.program_id(1)
    lo = lax.broadcasted_iota(jnp.int32, (BLK, LANES), 1) < HEAD_DIM
    first = lax.broadcasted_iota(jnp.int32, (2 * BLK, 1), 0) < BLK

    def per_kv_head(x):
        rolled = pltpu.roll(x, HEAD_DIM, axis=1)
        return jnp.where(lo, x, rolled).astype(BF16), jnp.where(lo, rolled, x).astype(BF16)

    k_blocks = [per_kv_head(kvp_ref[:, :B_KV])]
    v_blocks = [per_kv_head(kvp_ref[:, B_KV:])]
    for i in range(B_BLOCKS_PER_STEP):
        rows = slice(i * BLK, (i + 1) * BLK)
        k_blocks.append(per_kv_head(kvc_ref[rows, :B_KV]))
        v_blocks.append(per_kv_head(kvc_ref[rows, B_KV:]))
    for i in range(B_BLOCKS_PER_STEP):
        rows = slice(i * BLK, (i + 1) * BLK)
        has_prev = (j > 0) if i == 0 else True
        for pair in range(B_HEADS // 2):
            hk = (2 * pair) // B_GROUP
            sl = slice(pair * LANES, (pair + 1) * LANES)
            sink = jnp.where(first, sink_ref[2 * pair], sink_ref[2 * pair + 1])
            num, _, den = _pair_attend(_stack_pair(q_ref[rows, sl]), (k_blocks[i + 1][hk], v_blocks[i + 1][hk], None),
                                       (k_blocks[i][hk], v_blocks[i][hk], None), has_prev, sink)
            o_ref[rows, sl] = (num / den).astype(BF16)

    @pl.when(j == pl.num_programs(1) - 1)
    def _():
        c_ref[...] = kvc_ref[(B_BLOCKS_PER_STEP - 1) * BLK:, :].T


def _band_b(qb, kvb, sink):
    rows = B_BLOCKS_PER_STEP * BLK
    nb = SEQ // rows
    qv = qb.reshape(BATCH, SEQ, B_Q)
    kvv = kvb.reshape(BATCH, SEQ, 2 * B_KV)
    prev = lambda j: jnp.maximum(j * B_BLOCKS_PER_STEP - 1, 0)
    o, c = pl.pallas_call(
        _band_b_kernel,
        grid=(BATCH, nb),
        in_specs=[
            pl.BlockSpec(memory_space=pltpu.SMEM),
            pl.BlockSpec((None, rows, B_Q), lambda b, j: (b, j, 0)),
            pl.BlockSpec((None, BLK, 2 * B_KV), lambda b, j: (b, prev(j), 0)),
            pl.BlockSpec((None, rows, 2 * B_KV), lambda b, j: (b, j, 0)),
        ],
        out_specs=[
            pl.BlockSpec((None, rows, B_Q), lambda b, j: (b, j, 0)),
            pl.BlockSpec((None, 2 * B_KV, BLK), lambda b, j: (b, 0, 0)),
        ],
        out_shape=[
            jax.ShapeDtypeStruct((BATCH, SEQ, B_Q), BF16),
            jax.ShapeDtypeStruct((BATCH, 2 * B_KV, BLK), F32),
        ],
        compiler_params=_cparams("parallel", "arbitrary"),
        name="band_b",
    )(sink, qv, kvv, kvv)
    return o.reshape(BATCH * SEQ, B_Q), c


_N_HEADS = N_DIL * A_HEADS + B_HEADS
_COL_Q, _COL_K, _COL_V = 0, _N_HEADS, 2 * _N_HEADS
_COL_SINK_Q = N_DIL * A_HEADS


def _sample_attn_kernel(cols_ref, c0_ref, c1_ref, c2_ref, cb_ref, b0_ref, b1_ref, b2_ref, bnew_ref, sink_ref, o_ref):
    ns = SAMPLES_PER_STEP
    caches = (c0_ref, c1_ref, c2_ref)
    biases = (b0_ref, b1_ref, b2_ref)
    slab = cols_ref[...]
    k_new = pltpu.roll(slab, LANES - _COL_K, axis=2)
    v_new = pltpu.roll(slab, LANES - _COL_V, axis=2)
    s_new_row = jnp.sum(slab * k_new, axis=1, keepdims=True) + bnew_ref[...]
    lane_row = lax.broadcasted_iota(jnp.int32, (1, LANES), 1)
    lane = lax.broadcasted_iota(jnp.int32, (HEAD_DIM, LANES), 1)

    def per_head(x, start, count):
        return jnp.stack([jnp.stack([x[s, :, start + i:start + i + 1] for i in range(count)]) for s in range(ns)])

    def to_row(vals, start, rows):
        for s in range(ns):
            for i in range(vals.shape[1]):
                rows[s] = jnp.where(lane_row == start + i, vals[s, i], rows[s])

    def attend(start, count, kt_ref, vt_ref, bias_ref, sink):
        q = per_head(slab, start, count)
        s_new = per_head(s_new_row, start, count)
        tiles = []
        m_t = None
        for t in range(kt_ref.shape[-1] // LANES):
            sl = slice(t * LANES, (t + 1) * LANES)
            s = jnp.sum(kt_ref[:, :, :, sl] * q, axis=2, keepdims=True)
            if bias_ref is not None:
                bias = bias_ref[:, :, sl]
                s = jnp.where(bias > 0.5 * NEG_INF, s + bias, NEG_INF)
            tiles.append(s)
            m_t = s if m_t is None else jnp.maximum(m_t, s)
        m = jnp.maximum(jnp.max(m_t, axis=-1, keepdims=True), s_new)
        if sink is not None:
            m = jnp.maximum(m, sink)
        p_new = jnp.exp(s_new - m)
        den_t = jnp.zeros_like(m_t)
        pv_t = jnp.zeros((ns, count, HEAD_DIM, LANES), F32)
        for t, s in enumerate(tiles):
            p = jnp.exp(s - m)
            den_t = den_t + p
            pv_t = pv_t + vt_ref[:, :, :, t * LANES:(t + 1) * LANES] * p
        den = jnp.sum(den_t, axis=-1, keepdims=True) + p_new
        if sink is not None:
            den = den + jnp.exp(sink - m)
        return jnp.sum(pv_t, axis=-1, keepdims=True), m, p_new, den

    parts = [attend(g * A_HEADS, A_HEADS, caches[g].at[:, 0], caches[g].at[:, 1], biases[g], None)
             for g in range(N_DIL)]
    lses = [m + jnp.log(den) for _, m, _, den in parts]
    mx = jnp.maximum(jnp.maximum(lses[0], lses[1]), lses[2])
    es = [jnp.exp(lse - mx) for lse in lses]
    total = es[0] + es[1] + es[2]
    cache_a = jnp.zeros((ns, A_HEADS, HEAD_DIM, 1), F32)
    coef_rows = [jnp.zeros((1, LANES), F32) for _ in range(ns)]
    for g, (pv, _, p_new, den) in enumerate(parts):
        coef = es[g] / (total * den)
        cache_a = cache_a + coef * pv
        to_row(coef * p_new, g * A_HEADS, coef_rows)
    cache_b = []
    for hk in range(B_KV_HEADS):
        start = _COL_SINK_Q + hk * B_GROUP
        pv, _, p_new, den = attend(start, B_GROUP, cb_ref.at[:, 0, hk:hk + 1], cb_ref.at[:, 1, hk:hk + 1], None,
                                   sink_ref[hk * B_GROUP:(hk + 1) * B_GROUP])
        cache_b.append(pv / den)
        to_row(p_new / den, start, coef_rows)

    shift = lambda x, k: pltpu.roll(x, LANES - k, axis=1)
    for s in range(ns):
        out = jnp.zeros((HEAD_DIM, LANES), F32)
        for h in range(A_HEADS):
            out = jnp.where(lane == h, cache_a[s, h], out)
        for hq in range(B_HEADS):
            out = jnp.where(lane == A_HEADS + hq, cache_b[hq // B_GROUP][s, hq % B_GROUP], out)
        new = v_new[s] * coef_rows[s]
        groups = new + shift(new, A_HEADS) + shift(new, 2 * A_HEADS)
        o_ref[s] = out + jnp.where(lane < A_HEADS, groups, shift(new, 2 * A_HEADS))


def _sample_attn(h_a, qb, kvb, caches_t, l, bias_lanes, bias_new, sink):
    n = DEC_BATCH
    ha = h_a.reshape(n, N_DIL, 3, A_HEADS, HEAD_DIM)
    kb = jnp.repeat(kvb.reshape(n, 2, B_KV_HEADS, HEAD_DIM), B_GROUP, axis=2)
    flat = lambda part: ha[:, :, part].reshape(n, N_DIL * A_HEADS, HEAD_DIM)
    cols = jnp.concatenate([
        flat(0), qb.astype(F32).reshape(n, B_HEADS, HEAD_DIM), flat(1), kb[:, 0], flat(2), kb[:, 1],
        jnp.zeros((n, LANES - 3 * _N_HEADS, HEAD_DIM), F32)], axis=1)
    cols = cols.transpose(0, 2, 1)
    cache_spec = lambda c: pl.BlockSpec((None, SAMPLES_PER_STEP) + c.shape[2:], lambda b: (l, b, 0, 0, 0, 0))
    full = lambda a: pl.BlockSpec(a.shape, lambda b: (0,) * a.ndim)
    small = [a[:, None, :] for a in bias_lanes] + [bias_new, sink.reshape(B_HEADS, 1, 1)]
    out = pl.pallas_call(
        _sample_attn_kernel,
        grid=(n // SAMPLES_PER_STEP,),
        in_specs=[pl.BlockSpec((SAMPLES_PER_STEP, HEAD_DIM, LANES), lambda b: (b, 0, 0))]
        + [cache_spec(c) for c in caches_t] + [full(a) for a in small],
        out_specs=pl.BlockSpec((SAMPLES_PER_STEP, HEAD_DIM, LANES), lambda b: (b, 0, 0)),
        out_shape=jax.ShapeDtypeStruct((n, HEAD_DIM, LANES), F32),
        compiler_params=_cparams("parallel"),
        name="sample_attn",
    )(cols, *caches_t, *small)
    heads = out.transpose(0, 2, 1)
    out_a = heads[:, :A_HEADS].reshape(n, A_HD)
    out_b = heads[:, A_HEADS:A_HEADS + B_HEADS].reshape(n, B_Q)
    return out_a.astype(BF16), out_b.astype(BF16)


def _post_attn_kernel(a_ref, ob_ref, gate_ref, x_ref, mod_ref, wpa_ref, wpb_ref, wo_ref, lng_ref, lnb_ref, o_ref):
    ya = jnp.dot(a_ref[...], wpa_ref[...], preferred_element_type=F32)
    yb = jnp.dot(ob_ref[...], wpb_ref[...], preferred_element_type=F32)
    mix = gate_ref[:, :D_MODEL].astype(F32) * ya + gate_ref[:, D_MODEL:].astype(F32) * yb
    y = jnp.dot(mix.astype(BF16), wo_ref[...], preferred_element_type=F32)
    z = ALPHA * x_ref[...] + mod_ref[2] * y
    o_ref[...] = _layer_norm(z, lng_ref[0:1, :], lnb_ref[0:1, :])


def _post_attn(oa, ob, gates, x, mod, w_pa, w_pb, w_o, ln_g, ln_b, l, tm, rows_per_mod):
    t = x.shape[0]
    row = lambda w: pl.BlockSpec((tm, w), lambda i: (i, 0))
    const = lambda shape: pl.BlockSpec((None,) + shape, lambda i: (l, 0, 0), pipeline_mode=pl.Buffered(1))
    return pl.pallas_call(
        _post_attn_kernel,
        grid=(t // tm,),
        in_specs=[
            row(A_HD), row(B_Q), row(G_COLS), row(D_MODEL),
            _mod_spec(mod, tm, rows_per_mod),
            const((A_HD, D_MODEL)), const((B_Q, D_MODEL)), const((D_MODEL, D_MODEL)),
            const((2, D_MODEL)), const((2, D_MODEL)),
        ],
        out_specs=row(D_MODEL),
        out_shape=jax.ShapeDtypeStruct((t, D_MODEL), F32),
        compiler_params=_cparams("parallel"),
        name="post_attn",
    )(oa, ob, gates, x, mod, w_pa, w_pb, w_o, ln_g, ln_b)


def _ffn_kernel(x_ref, mod_ref, wg_ref, wu_ref, wd_ref, lng_ref, lnb_ref, o_ref, u_ref, acc_ref):
    k = pl.program_id(1)

    @pl.when(k == 0)
    def _():
        u_ref[...] = _modulate(x_ref, mod_ref, 3, 4)
        acc_ref[...] = jnp.zeros_like(acc_ref)

    u = u_ref[...]
    gate = jnp.dot(u, wg_ref[...], preferred_element_type=F32)
    up = jnp.dot(u, wu_ref[...], preferred_element_type=F32)
    act = (gate * jax.nn.sigmoid(gate) * up).astype(BF16)
    acc_ref[...] += jnp.dot(act, wd_ref[...], preferred_element_type=F32)

    @pl.when(k == pl.num_programs(1) - 1)
    def _():
        z = ALPHA * x_ref[...] + mod_ref[5] * acc_ref[...]
        o_ref[...] = _layer_norm(z, lng_ref[1:2, :], lnb_ref[1:2, :])


def _ffn(x, mod, w_gu, w_down, ln_g, ln_b, l, tm, rows_per_mod):
    t = x.shape[0]
    tf = 512
    nf = D_FF // tf
    return pl.pallas_call(
        _ffn_kernel,
        grid=(t // tm, nf),
        in_specs=[
            pl.BlockSpec((tm, D_MODEL), lambda i, k: (i, 0)),
            _mod_spec(mod, tm, rows_per_mod),
            pl.BlockSpec((None, D_MODEL, tf), lambda i, k: (l, 0, k)),
            pl.BlockSpec((None, D_MODEL, tf), lambda i, k: (l, 0, k + nf)),
            pl.BlockSpec((None, tf, D_MODEL), lambda i, k: (l, k, 0)),
            pl.BlockSpec((None, 2, D_MODEL), lambda i, k: (l, 0, 0)),
            pl.BlockSpec((None, 2, D_MODEL), lambda i, k: (l, 0, 0)),
        ],
        out_specs=pl.BlockSpec((tm, D_MODEL), lambda i, k: (i, 0)),
        out_shape=jax.ShapeDtypeStruct((t, D_MODEL), F32),
        scratch_shapes=[pltpu.VMEM((tm, D_MODEL), BF16), pltpu.VMEM((tm, D_MODEL), F32)],
        compiler_params=_cparams("parallel", "arbitrary"),
        name="ffn",
    )(x, mod, w_gu, w_gu, w_down, ln_g, ln_b)


def _t5_bucket(dist):
    exact = N_BUCKETS // 2
    n = jnp.maximum(dist, 0)
    log_ratio = jnp.log(jnp.maximum(n, exact).astype(F32) / exact) / math.log(T5_MAX_DIST / exact)
    large = jnp.minimum(exact + (log_ratio * (N_BUCKETS - exact)).astype(jnp.int32), N_BUCKETS - 1)
    return jnp.where(n < exact, n, large)


def _bias_tables(t5_table):
    tps, tcs, lanes, news = [], [], [], []
    for g, (w, d) in enumerate(DILATIONS):
        bucket = _t5_bucket(d * jnp.arange(BLK + 1))
        onehot = (bucket[:, None] == jnp.arange(N_BUCKETS)[None, :]).astype(F32)
        bias = jnp.dot(onehot, t5_table[:, g * A_HEADS:(g + 1) * A_HEADS],
                       precision=lax.Precision.HIGHEST).T
        wv = jnp.concatenate([bias[:, ::-1], jnp.zeros((A_HEADS, BLK), F32)], axis=1)
        toep = jnp.tile(wv, (1, BLK))[:, :BLK * 2 * BLK].reshape(A_HEADS, BLK, 2 * BLK)
        tps.append(toep[:, :, :BLK])
        tcs.append(toep[:, :, BLK:])
        by_row = jnp.repeat(bias[:, BLK:0:-1], d, axis=1)
        t = jnp.arange(BLK * d)[None, :]
        lanes.append(jnp.where(t % d == 0, by_row, NEG_INF))
        news.append(bias[:, 0])
    new = jnp.concatenate(news + [jnp.zeros((LANES - N_DIL * A_HEADS,), F32)])[None, :]
    return jnp.stack(tps), jnp.stack(tcs), lanes, new


def _rope_tables(pos):
    half = HEAD_DIM // 2
    inv = ROPE_THETA ** (-jnp.arange(half, dtype=F32) / half)
    ang = pos.astype(F32)[:, None] * inv[None]
    cos, sin = jnp.cos(ang), jnp.sin(ang)
    cos_t = jnp.concatenate([cos, cos, cos, cos], axis=-1)
    sin_t = jnp.concatenate([-sin, sin, -sin, sin], axis=-1)
    return cos_t, sin_t


def kernel(x_prompt, x_sample, c_prompt, c_sample, cache_a0, cache_a1, cache_a2, cache_b, t5_table, w_ada, b_ada,
           w_in, sinks, w_pa, w_pb, w_o, w_gu, w_down, ln_g, ln_b):
    tp_rows = BATCH * SEQ
    mod = _adaln(jnp.concatenate([c_prompt, c_sample], axis=0), w_ada, b_ada)
    mod_p = mod[:, :BATCH].reshape(DEPTH, BATCH, 6, 1, D_MODEL).transpose(0, 2, 1, 3, 4)
    mod_s = mod[:, BATCH:].reshape(DEPTH, 1, DEC_BATCH, 6, D_MODEL).transpose(0, 3, 1, 2, 4)

    q_scale = jnp.concatenate([jnp.full((A_HD,), Q_SCALE, F32), jnp.ones((2 * A_HD,), F32)])
    col_scale = jnp.concatenate([jnp.tile(q_scale, N_DIL), jnp.full((B_Q,), Q_SCALE, F32),
                                 jnp.ones((2 * B_KV + G_COLS,), F32)])
    w_a = w_b = w_g = (w_in * col_scale).astype(BF16)
    w_pa_h, w_pb_h, w_o_h = w_pa.astype(BF16), w_pb.astype(BF16), w_o.astype(BF16)
    w_gu_h, w_down_h = w_gu.astype(BF16), w_down.astype(BF16)

    tps, tcs, bias_lanes, bias_new = _bias_tables(t5_table)
    cos_p, sin_p = _rope_tables(jnp.arange(SEQ))
    cos_s, sin_s = _rope_tables(jnp.full((DEC_BATCH,), PAST_LEN))
    caches_t = [c.transpose(0, 1, 3, 4, 5, 2) for c in (cache_a0, cache_a1, cache_a2, cache_b)]

    xp = x_prompt.reshape(tp_rows, D_MODEL)
    xs = x_sample.reshape(DEC_BATCH, D_MODEL)
    kv_rows = None
    rows_a_s = [[] for _ in range(N_DIL)]
    rows_b_p, rows_b_s = [], []
    for l in range(DEPTH):
        sink = sinks[l].reshape(B_HEADS)

        h_a = _proj_a(xp, mod_p[l], w_a, l, ROW_TILE_PROJ, SEQ)
        qb, kvb = _proj_b(xp, mod_p[l], w_b, l, cos_p, sin_p, ROW_TILE_PROJ, SEQ)
        gates = _gates(xp, mod_p[l], w_g, l, ROW_TILE_PROJ, SEQ)
        oa, kv_rows = _dil_attn(h_a, tps, tcs, l, kv_rows)
        ob, b_rows = _band_b(qb, kvb, sink)
        xp = _post_attn(oa, ob, gates, xp, mod_p[l], w_pa_h, w_pb_h, w_o_h, ln_g, ln_b, l, ROW_TILE_POST, SEQ)
        xp = _ffn(xp, mod_p[l], w_gu_h, w_down_h, ln_g, ln_b, l, ROW_TILE_FFN, SEQ)
        rows_b_p.append(b_rows.reshape(BATCH, 2, B_KV_HEADS, HEAD_DIM, BLK))

        h_a = _proj_a(xs, mod_s[l], w_a, l, DEC_BATCH, DEC_BATCH)
        qb, kvb = _proj_b(xs, mod_s[l], w_b, l, cos_s, sin_s, DEC_BATCH, DEC_BATCH)
        gates = _gates(xs, mod_s[l], w_g, l, DEC_BATCH, DEC_BATCH)
        oa, ob = _sample_attn(h_a, qb, kvb, caches_t, l, bias_lanes, bias_new, sink)
        xs = _post_attn(oa, ob, gates, xs, mod_s[l], w_pa_h, w_pb_h, w_o_h, ln_g, ln_b, l, DEC_BATCH, DEC_BATCH)
        xs = _ffn(xs, mod_s[l], w_gu_h, w_down_h, ln_g, ln_b, l, DEC_BATCH, DEC_BATCH)
        ha = h_a.reshape(DEC_BATCH, 1, N_DIL, 3, A_HEADS, HEAD_DIM)
        for g in range(N_DIL):
            rows_a_s[g].append(ha[:, :, g, 1:])
        rows_b_s.append(kvb.reshape(DEC_BATCH, 1, 2, B_KV_HEADS, HEAD_DIM))

    to_rows_major = lambda a: a.transpose(0, 1, 5, 2, 3, 4)
    outs = [xp.reshape(BATCH, SEQ, D_MODEL), xs.reshape(DEC_BATCH, 1, D_MODEL)]
    for g in range(N_DIL):
        outs.append(to_rows_major(kv_rows[g]))
        outs.append(jnp.stack(rows_a_s[g]))
    outs.append(to_rows_major(jnp.stack(rows_b_p)))
    outs.append(jnp.stack(rows_b_s))
    return tuple(outs)
```

```python
import functools
import math

import jax
import jax.numpy as jnp
from jax import lax
from jax.experimental import pallas as pl
from jax.experimental.pallas import tpu as pltpu

F32 = jnp.float32
BF16 = jnp.bfloat16

D_MODEL = 2048
BATCH = 8
SEQ = 2048
DEPTH = 2
DEC_BATCH = 128
PAST_LEN = 8192
HEAD_DIM = 64
DILATIONS = ((128, 1), (512, 4), (2048, 16))
N_DIL = 3
A_HEADS = 8
B_HEADS = 16
B_KV_HEADS = 2
B_GROUP = B_HEADS // B_KV_HEADS
B_WINDOW = 128
ROPE_THETA = 150000.0
N_BUCKETS = 32
T5_MAX_DIST = 2048
D_FF = ((8 * D_MODEL + 3 * 256 - 1) // (3 * 256)) * 256
ALPHA = (2 * DEPTH) ** 0.25
LN_EPS = 1e-5
NEG_INF = -1e30

BLK = 128
A_HD = A_HEADS * HEAD_DIM
A_GROUP_COLS = 3 * A_HD
A_COLS = N_DIL * A_GROUP_COLS
B_Q = B_HEADS * HEAD_DIM
B_KV = B_KV_HEADS * HEAD_DIM
B_COLS = B_Q + 2 * B_KV
G_COLS = 2 * D_MODEL
LANES = 128
PAIRS_PER_STEP = 2
ROW_TILE_PROJ = 1024
ROW_TILE_POST = 256
ROW_TILE_FFN = 512
B_BLOCKS_PER_STEP = 8
SAMPLES_PER_STEP = 2
UNIT_UNROLL = 8
Q_SCALE = HEAD_DIM ** -0.5
A_KEEP = tuple(min(w, SEQ) for w, _ in DILATIONS)

V7X_VMEM_BYTES = 64 * 1024 * 1024
_VMEM_LIMIT = V7X_VMEM_BYTES * 7 // 8


def _cparams(*sem):
    return pltpu.CompilerParams(dimension_semantics=sem, vmem_limit_bytes=_VMEM_LIMIT)


def _mod_spec(mod, tm, rows_per_mod):
    mr = mod.shape[2]
    return pl.BlockSpec((6, None, mr, D_MODEL), lambda i, *_: (0, (i * tm) // rows_per_mod, 0, 0))


def _layer_norm(z, g, b):
    mu = jnp.mean(z, axis=-1, keepdims=True)
    zc = z - mu
    var = jnp.mean(zc * zc, axis=-1, keepdims=True)
    return zc * lax.rsqrt(var + LN_EPS) * g + b


def _adaln_kernel(c_ref, w_ref, b_ref, o_ref):
    c = c_ref[...]
    s = (c * jax.nn.sigmoid(c)).astype(BF16)
    o_ref[...] = jnp.dot(s, w_ref[...].astype(BF16), preferred_element_type=F32) + b_ref[...]


def _adaln(c_all, w_ada, b_ada):
    rows = c_all.shape[0]
    tn = 1024
    return pl.pallas_call(
        _adaln_kernel,
        grid=(DEPTH, 6 * D_MODEL // tn),
        in_specs=[
            pl.BlockSpec((rows, D_MODEL), lambda l, j: (0, 0)),
            pl.BlockSpec((None, D_MODEL, tn), lambda l, j: (l, 0, j)),
            pl.BlockSpec((None, 1, tn), lambda l, j: (l, 0, j)),
        ],
        out_specs=pl.BlockSpec((None, rows, tn), lambda l, j: (l, 0, j)),
        out_shape=jax.ShapeDtypeStruct((DEPTH, rows, 6 * D_MODEL), F32),
        compiler_params=_cparams("parallel", "parallel"),
        name="adaln",
    )(c_all, w_ada, b_ada.reshape(DEPTH, 1, 6 * D_MODEL))


def _modulate(x_ref, mod_ref, shift_idx, scale_idx):
    return (x_ref[...] * (1.0 + mod_ref[scale_idx]) + mod_ref[shift_idx]).astype(BF16)


def _proj_a_kernel(x_ref, mod_ref, w_ref, o_ref, u_ref):
    @pl.when(pl.program_id(1) == 0)
    def _():
        u_ref[...] = _modulate(x_ref, mod_ref, 0, 1)

    o_ref[...] = jnp.dot(u_ref[...], w_ref[...], preferred_element_type=F32)


def _proj_a(x, mod, w_a, l, tm, rows_per_mod):
    t = x.shape[0]
    return pl.pallas_call(
        _proj_a_kernel,
        grid=(t // tm, N_DIL),
        in_specs=[
            pl.BlockSpec((tm, D_MODEL), lambda i, g: (i, 0)),
            _mod_spec(mod, tm, rows_per_mod),
            pl.BlockSpec((None, D_MODEL, A_GROUP_COLS), lambda i, g: (l, 0, g)),
        ],
        out_specs=pl.BlockSpec((tm, A_GROUP_COLS), lambda i, g: (i, g)),
        out_shape=jax.ShapeDtypeStruct((t, A_COLS), F32),
        scratch_shapes=[pltpu.VMEM((tm, D_MODEL), BF16)],
        compiler_params=_cparams("parallel", "arbitrary"),
        name="proj_a",
    )(x, mod, w_a)


def _proj_b_kernel(x_ref, mod_ref, w_ref, cos_ref, sin_ref, q_ref, kv_ref):
    u = _modulate(x_ref, mod_ref, 0, 1)
    acc = jnp.dot(u, w_ref[0], preferred_element_type=F32)
    cos = cos_ref[...]
    sin = sin_ref[...]
    lane = lax.broadcasted_iota(jnp.int32, cos.shape, 1)
    first_half = (lane % HEAD_DIM) < (HEAD_DIM // 2)

    def rope(v):
        partner = jnp.where(first_half,
                            pltpu.roll(v, LANES - HEAD_DIM // 2, axis=1),
                            pltpu.roll(v, HEAD_DIM // 2, axis=1))
        return v * cos + partner * sin

    for c in range(B_Q // LANES):
        q_ref[:, c * LANES:(c + 1) * LANES] = rope(acc[:, c * LANES:(c + 1) * LANES]).astype(BF16)
    kv_ref[:, :B_KV] = rope(acc[:, B_Q:B_Q + B_KV])
    kv_ref[:, B_KV:] = acc[:, B_Q + B_KV:]


def _proj_b(x, mod, w_b, l, cos_t, sin_t, tm, rows_per_mod):
    t = x.shape[0]
    n_pos_blocks = cos_t.shape[0] // tm
    return pl.pallas_call(
        _proj_b_kernel,
        grid=(t // tm,),
        in_specs=[
            pl.BlockSpec((tm, D_MODEL), lambda i: (i, 0)),
            _mod_spec(mod, tm, rows_per_mod),
            pl.BlockSpec((pl.Element(1), pl.Element(D_MODEL), pl.Element(B_COLS)), lambda i: (l, 0, A_COLS)),
            pl.BlockSpec((tm, LANES), lambda i: (i % n_pos_blocks, 0)),
            pl.BlockSpec((tm, LANES), lambda i: (i % n_pos_blocks, 0)),
        ],
        out_specs=[
            pl.BlockSpec((tm, B_Q), lambda i: (i, 0)),
            pl.BlockSpec((tm, 2 * B_KV), lambda i: (i, 0)),
        ],
        out_shape=[
            jax.ShapeDtypeStruct((t, B_Q), BF16),
            jax.ShapeDtypeStruct((t, 2 * B_KV), F32),
        ],
        compiler_params=_cparams("parallel"),
        name="proj_b",
    )(x, mod, w_b, cos_t, sin_t)


def _gates_kernel(x_ref, mod_ref, w_ref, o_ref, u_ref):
    @pl.when(pl.program_id(1) == 0)
    def _():
        u_ref[...] = _modulate(x_ref, mod_ref, 0, 1)

    acc = jnp.dot(u_ref[...], w_ref[0], preferred_element_type=F32)
    o_ref[...] = jax.nn.sigmoid(acc).astype(o_ref.dtype)


def _gates(x, mod, w_g, l, tm, rows_per_mod):
    t = x.shape[0]
    tn = 2048
    return pl.pallas_call(
        _gates_kernel,
        grid=(t // tm, G_COLS // tn),
        in_specs=[
            pl.BlockSpec((tm, D_MODEL), lambda i, j: (i, 0)),
            _mod_spec(mod, tm, rows_per_mod),
            pl.BlockSpec((pl.Element(1), pl.Element(D_MODEL), pl.Element(tn)),
                         lambda i, j: (l, 0, pl.multiple_of(A_COLS + B_COLS + j * tn, LANES))),
        ],
        out_specs=pl.BlockSpec((tm, tn), lambda i, j: (i, j)),
        out_shape=jax.ShapeDtypeStruct((t, G_COLS), BF16),
        scratch_shapes=[pltpu.VMEM((tm, D_MODEL), BF16)],
        compiler_params=_cparams("parallel", "arbitrary"),
        name="gates",
    )(x, mod, w_g)


def _dot_nt(a, b):
    return lax.dot_general(a, b, (((1,), (1,)), ((), ())), preferred_element_type=F32)


def _band_masks():
    row = lax.broadcasted_iota(jnp.int32, (2 * BLK, BLK), 0) % BLK
    col = lax.broadcasted_iota(jnp.int32, (2 * BLK, BLK), 1)
    return col >= row, col <= row


def _rows(start, d):
    return pl.ds(start, BLK, stride=d) if d > 1 else pl.ds(pl.multiple_of(start, BLK), BLK)


def _stack_pair(q):
    lo = lax.broadcasted_iota(jnp.int32, (BLK, LANES), 1) < HEAD_DIM
    zero = jnp.zeros_like(q)
    return jnp.concatenate([jnp.where(lo, q, zero), jnp.where(lo, zero, q)], axis=0)


def _unstack_pair(x):
    lo = lax.broadcasted_iota(jnp.int32, (BLK, LANES), 1) < HEAD_DIM
    return jnp.where(lo, jnp.broadcast_to(x[:BLK], (BLK, LANES)), jnp.broadcast_to(x[BLK:], (BLK, LANES)))


def _pair_attend(qs, cur, prev, has_prev, sink):
    valid_p, valid_c = _band_masks()

    def scores(k, bias, valid):
        s = _dot_nt(qs, k)
        if bias is not None:
            s = s + bias
        return jnp.where(valid, s, NEG_INF)

    sc = scores(cur[0], cur[2], valid_c)
    if prev is not None:
        sp = jnp.where(has_prev, scores(prev[0], prev[2], valid_p), NEG_INF)
        m = jnp.max(jnp.maximum(sp, sc), axis=-1, keepdims=True)
    else:
        m = jnp.max(sc, axis=-1, keepdims=True)
    if sink is not None:
        m = jnp.maximum(m, sink)
    pc = jnp.exp(sc - m)
    pv = jnp.dot(pc.astype(BF16), cur[1], preferred_element_type=F32)
    if prev is not None:
        pp = jnp.exp(sp - m)
        pv = pv + jnp.dot(pp.astype(BF16), prev[1], preferred_element_type=F32)
        den = jnp.sum(pp + pc, axis=-1, keepdims=True)
    else:
        den = jnp.sum(pc, axis=-1, keepdims=True)
    if sink is not None:
        den = den + jnp.exp(sink - m)
    return _unstack_pair(pv), _unstack_pair(m), _unstack_pair(den)


def _dil_group(gi, q_refs, k_refs, v_refs, tp_ref, tc_ref, acc_ref, m_ref, l_ref):
    d = DILATIONS[gi][1]
    nblk = SEQ // (BLK * d)

    def unit(u, carry):
        r = u // nblk
        j = u % nblk
        start = j * (BLK * d) + r
        start_prev = jnp.maximum(j - 1, 0) * (BLK * d) + r
        rows = _rows(start, d)
        rows_prev = _rows(start_prev, d)
        for p in range(PAIRS_PER_STEP):
            qs = _stack_pair(q_refs[p][rows, :].astype(BF16))
            cur = (k_refs[p][rows, :].astype(BF16), v_refs[p][rows, :].astype(BF16),
                   tc_ref[2 * p:2 * p + 2].reshape(2 * BLK, BLK))
            prev = None
            if nblk > 1:
                prev = (k_refs[p][rows_prev, :].astype(BF16), v_refs[p][rows_prev, :].astype(BF16),
                        tp_ref[2 * p:2 * p + 2].reshape(2 * BLK, BLK))
            num, m_g, l_g = _pair_attend(qs, cur, prev, j > 0, None)
            if _group_step(gi) == 0:
                acc_ref[p, rows, :] = num
                m_ref[p, rows, :] = m_g
                l_ref[p, rows, :] = l_g
            else:
                m_old = m_ref[p, rows, :]
                m_new = jnp.maximum(m_old, m_g)
                a = jnp.exp(m_old - m_new)
                b = jnp.exp(m_g - m_new)
                acc_ref[p, rows, :] = acc_ref[p, rows, :] * a + num * b
                l_ref[p, rows, :] = l_ref[p, rows, :] * a + l_g * b
                m_ref[p, rows, :] = m_new
        return carry

    lax.fori_loop(0, SEQ // BLK, unit, 0, unroll=UNIT_UNROLL)


def _group_step(gi):
    return N_DIL - 1 - gi


def _dil_attn_kernel(q0, q1, k0, k1, v0, v1, tp_ref, tc_ref, *rest, layer, first):
    o_ref, c0_ref, c1_ref, c2_ref, acc_ref, m_ref, l_ref = rest[-7:]
    g = pl.program_id(2)
    q_refs, k_refs, v_refs = (q0, q1), (k0, k1), (v0, v1)
    c_refs = (c0_ref, c1_ref, c2_ref)
    for gi in range(N_DIL):
        @pl.when(g == _group_step(gi))
        def _(gi=gi):
            _dil_group(gi, q_refs, k_refs, v_refs, tp_ref, tc_ref, acc_ref, m_ref, l_ref)
            keep = A_KEEP[gi]
            c_ref = c_refs[gi].at[layer] if first else c_refs[gi]
            for p in range(PAIRS_PER_STEP):
                c_ref[0, 2 * p:2 * p + 2] = k_refs[p][SEQ - keep:, :].T.reshape(2, HEAD_DIM, keep)
                c_ref[1, 2 * p:2 * p + 2] = v_refs[p][SEQ - keep:, :].T.reshape(2, HEAD_DIM, keep)
            if first:
                for later in range(layer + 1, DEPTH):
                    c_refs[gi][later] = jnp.zeros(c_refs[gi].shape[1:], F32)

    @pl.when(g == N_DIL - 1)
    def _():
        for p in range(PAIRS_PER_STEP):
            o_ref[:, p * LANES:(p + 1) * LANES] = (acc_ref[p] / l_ref[p]).astype(BF16)


def _dil_attn(h_a, tps, tcs, l, kv_rows):
    hv = h_a.reshape(BATCH, SEQ, A_COLS)
    steps = A_HEADS // 2 // PAIRS_PER_STEP
    heads = 2 * PAIRS_PER_STEP
    slab = (None, SEQ, LANES)
    col = lambda part, p: (lambda b, s, g: (b, 0, _group_step(g) * 12 + part * 4 + s * PAIRS_PER_STEP + p))
    tbl = pl.BlockSpec((None, heads, BLK, BLK), lambda b, s, g: (_group_step(g), s, 0, 0))
    first = kv_rows is None
    cache = lambda keep: pl.BlockSpec((DEPTH if first else None, None, 2, heads, HEAD_DIM, keep),
                                      lambda b, s, g: (0 if first else l, b, 0, s, 0, 0))
    in_specs = [pl.BlockSpec(slab, col(part, p)) for part in range(3) for p in range(PAIRS_PER_STEP)] + [tbl, tbl]
    operands = [hv] * (3 * PAIRS_PER_STEP) + [tps, tcs]
    aliases = {}
    if not first:
        aliases = {len(operands) + i: 1 + i for i in range(N_DIL)}
        in_specs += [pl.BlockSpec(memory_space=pl.ANY)] * N_DIL
        operands += list(kv_rows)
    outs = pl.pallas_call(
        functools.partial(_dil_attn_kernel, layer=l, first=first),
        grid=(BATCH, steps, N_DIL),
        in_specs=in_specs,
        out_specs=[pl.BlockSpec((None, SEQ, heads * HEAD_DIM), lambda b, s, g: (b, 0, s))]
        + [cache(keep) for keep in A_KEEP],
        out_shape=[jax.ShapeDtypeStruct((BATCH, SEQ, A_HD), BF16)]
        + [jax.ShapeDtypeStruct((DEPTH, BATCH, 2, A_HEADS, HEAD_DIM, keep), F32) for keep in A_KEEP],
        scratch_shapes=[pltpu.VMEM((PAIRS_PER_STEP, SEQ, LANES), F32)] * 3,
        input_output_aliases=aliases,
        compiler_params=_cparams("parallel", "parallel", "arbitrary"),
        name="dil_attn",
    )(*operands)
    return outs[0].reshape(BATCH * SEQ, A_HD), outs[1:]


def _band_b_kernel(sink_ref, q_ref, kvp_ref, kvc_ref, o_ref, c_ref):
    j = pl.program_id(1)
    lo = lax.broadcasted_iota(jnp.int32, (BLK, LANES), 1) < HEAD_DIM
    first = lax.broadcasted_iota(jnp.int32, (2 * BLK, 1), 0) < BLK

    def per_kv_head(x):
        rolled = pltpu.roll(x, HEAD_DIM, axis=1)
        return jnp.where(lo, x, rolled).astype(BF16), jnp.where(lo, rolled, x).astype(BF16)

    k_blocks = [per_kv_head(kvp_ref[:, :B_KV])]
    v_blocks = [per_kv_head(kvp_ref[:, B_KV:])]
    for i in range(B_BLOCKS_PER_STEP):
        rows = slice(i * BLK, (i + 1) * BLK)
        k_blocks.append(per_kv_head(kvc_ref[rows, :B_KV]))
        v_blocks.append(per_kv_head(kvc_ref[rows, B_KV:]))
    for i in range(B_BLOCKS_PER_STEP):
        rows = slice(i * BLK, (i + 1) * BLK)
        has_prev = (j > 0) if i == 0 else True
        for pair in range(B_HEADS // 2):
            hk = (2 * pair) // B_GROUP
            sl = slice(pair * LANES, (pair + 1) * LANES)
            sink = jnp.where(first, sink_ref[2 * pair], sink_ref[2 * pair + 1])
            num, _, den = _pair_attend(_stack_pair(q_ref[rows, sl]), (k_blocks[i + 1][hk], v_blocks[i + 1][hk], None),
                                       (k_blocks[i][hk], v_blocks[i][hk], None), has_prev, sink)
            o_ref[rows, sl] = (num / den).astype(BF16)

    @pl.when(j == pl.num_programs(1) - 1)
    def _():
        c_ref[...] = kvc_ref[(B_BLOCKS_PER_STEP - 1) * BLK:, :].T


def _band_b(qb, kvb, sink):
    rows = B_BLOCKS_PER_STEP * BLK
    nb = SEQ // rows
    qv = qb.reshape(BATCH, SEQ, B_Q)
    kvv = kvb.reshape(BATCH, SEQ, 2 * B_KV)
    prev = lambda j: jnp.maximum(j * B_BLOCKS_PER_STEP - 1, 0)
    o, c = pl.pallas_call(
        _band_b_kernel,
        grid=(BATCH, nb),
        in_specs=[
            pl.BlockSpec(memory_space=pltpu.SMEM),
            pl.BlockSpec((None, rows, B_Q), lambda b, j: (b, j, 0)),
            pl.BlockSpec((None, BLK, 2 * B_KV), lambda b, j: (b, prev(j), 0)),
            pl.BlockSpec((None, rows, 2 * B_KV), lambda b, j: (b, j, 0)),
        ],
        out_specs=[
            pl.BlockSpec((None, rows, B_Q), lambda b, j: (b, j, 0)),
            pl.BlockSpec((None, 2 * B_KV, BLK), lambda b, j: (b, 0, 0)),
        ],
        out_shape=[
            jax.ShapeDtypeStruct((BATCH, SEQ, B_Q), BF16),
            jax.ShapeDtypeStruct((BATCH, 2 * B_KV, BLK), F32),
        ],
        compiler_params=_cparams("parallel", "arbitrary"),
        name="band_b",
    )(sink, qv, kvv, kvv)
    return o.reshape(BATCH * SEQ, B_Q), c


_N_HEADS = N_DIL * A_HEADS + B_HEADS
_COL_Q, _COL_K, _COL_V = 0, _N_HEADS, 2 * _N_HEADS
_COL_SINK_Q = N_DIL * A_HEADS


def _sample_attn_kernel(cols_ref, c0_ref, c1_ref, c2_ref, cb_ref, b0_ref, b1_ref, b2_ref, bnew_ref, sink_ref, o_ref):
    ns = SAMPLES_PER_STEP
    caches = (c0_ref, c1_ref, c2_ref)
    biases = (b0_ref, b1_ref, b2_ref)
    slab = cols_ref[...]
    k_new = pltpu.roll(slab, LANES - _COL_K, axis=2)
    v_new = pltpu.roll(slab, LANES - _COL_V, axis=2)
    s_new_row = jnp.sum(slab * k_new, axis=1, keepdims=True) + bnew_ref[...]
    lane_row = lax.broadcasted_iota(jnp.int32, (1, LANES), 1)
    lane = lax.broadcasted_iota(jnp.int32, (HEAD_DIM, LANES), 1)

    def per_head(x, start, count):
        return jnp.stack([jnp.stack([x[s, :, start + i:start + i + 1] for i in range(count)]) for s in range(ns)])

    def to_row(vals, start, rows):
        for s in range(ns):
            for i in range(vals.shape[1]):
                rows[s] = jnp.where(lane_row == start + i, vals[s, i], rows[s])

    def attend(start, count, kt_ref, vt_ref, bias_ref, sink):
        q = per_head(slab, start, count)
        s_new = per_head(s_new_row, start, count)
        tiles = []
        m_t = None
        for t in range(kt_ref.shape[-1] // LANES):
            sl = slice(t * LANES, (t + 1) * LANES)
            s = jnp.sum(kt_ref[:, :, :, sl] * q, axis=2, keepdims=True)
            if bias_ref is not None:
                bias = bias_ref[:, :, sl]
                s = jnp.where(bias > 0.5 * NEG_INF, s + bias, NEG_INF)
            tiles.append(s)
            m_t = s if m_t is None else jnp.maximum(m_t, s)
        m = jnp.maximum(jnp.max(m_t, axis=-1, keepdims=True), s_new)
        if sink is not None:
            m = jnp.maximum(m, sink)
        p_new = jnp.exp(s_new - m)
        den_t = jnp.zeros_like(m_t)
        pv_t = jnp.zeros((ns, count, HEAD_DIM, LANES), F32)
        for t, s in enumerate(tiles):
            p = jnp.exp(s - m)
            den_t = den_t + p
            pv_t = pv_t + vt_ref[:, :, :, t * LANES:(t + 1) * LANES] * p
        den = jnp.sum(den_t, axis=-1, keepdims=True) + p_new
        if sink is not None:
            den = den + jnp.exp(sink - m)
        return jnp.sum(pv_t, axis=-1, keepdims=True), m, p_new, den

    parts = [attend(g * A_HEADS, A_HEADS, caches[g].at[:, 0], caches[g].at[:, 1], biases[g], None)
             for g in range(N_DIL)]
    lses = [m + jnp.log(den) for _, m, _, den in parts]
    mx = jnp.maximum(jnp.maximum(lses[0], lses[1]), lses[2])
    es = [jnp.exp(lse - mx) for lse in lses]
    total = es[0] + es[1] + es[2]
    cache_a = jnp.zeros((ns, A_HEADS, HEAD_DIM, 1), F32)
    coef_rows = [jnp.zeros((1, LANES), F32) for _ in range(ns)]
    for g, (pv, _, p_new, den) in enumerate(parts):
        coef = es[g] / (total * den)
        cache_a = cache_a + coef * pv
        to_row(coef * p_new, g * A_HEADS, coef_rows)
    cache_b = []
    for hk in range(B_KV_HEADS):
        start = _COL_SINK_Q + hk * B_GROUP
        pv, _, p_new, den = attend(start, B_GROUP, cb_ref.at[:, 0, hk:hk + 1], cb_ref.at[:, 1, hk:hk + 1], None,
                                   sink_ref[hk * B_GROUP:(hk + 1) * B_GROUP])
        cache_b.append(pv / den)
        to_row(p_new / den, start, coef_rows)

    shift = lambda x, k: pltpu.roll(x, LANES - k, axis=1)
    for s in range(ns):
        out = jnp.zeros((HEAD_DIM, LANES), F32)
        for h in range(A_HEADS):
            out = jnp.where(lane == h, cache_a[s, h], out)
        for hq in range(B_HEADS):
            out = jnp.where(lane == A_HEADS + hq, cache_b[hq // B_GROUP][s, hq % B_GROUP], out)
        new = v_new[s] * coef_rows[s]
        groups = new + shift(new, A_HEADS) + shift(new, 2 * A_HEADS)
        o_ref[s] = out + jnp.where(lane < A_HEADS, groups, shift(new, 2 * A_HEADS))


def _sample_attn(h_a, qb, kvb, caches_t, l, bias_lanes, bias_new, sink):
    n = DEC_BATCH
    ha = h_a.reshape(n, N_DIL, 3, A_HEADS, HEAD_DIM)
    kb = jnp.repeat(kvb.reshape(n, 2, B_KV_HEADS, HEAD_DIM), B_GROUP, axis=2)
    flat = lambda part: ha[:, :, part].reshape(n, N_DIL * A_HEADS, HEAD_DIM)
    cols = jnp.concatenate([
        flat(0), qb.astype(F32).reshape(n, B_HEADS, HEAD_DIM), flat(1), kb[:, 0], flat(2), kb[:, 1],
        jnp.zeros((n, LANES - 3 * _N_HEADS, HEAD_DIM), F32)], axis=1)
    cols = cols.transpose(0, 2, 1)
    cache_spec = lambda c: pl.BlockSpec((None, SAMPLES_PER_STEP) + c.shape[2:], lambda b: (l, b, 0, 0, 0, 0))
    full = lambda a: pl.BlockSpec(a.shape, lambda b: (0,) * a.ndim)
    small = [a[:, None, :] for a in bias_lanes] + [bias_new, sink.reshape(B_HEADS, 1, 1)]
    out = pl.pallas_call(
        _sample_attn_kernel,
        grid=(n // SAMPLES_PER_STEP,),
        in_specs=[pl.BlockSpec((SAMPLES_PER_STEP, HEAD_DIM, LANES), lambda b: (b, 0, 0))]
        + [cache_spec(c) for c in caches_t] + [full(a) for a in small],
        out_specs=pl.BlockSpec((SAMPLES_PER_STEP, HEAD_DIM, LANES), lambda b: (b, 0, 0)),
        out_shape=jax.ShapeDtypeStruct((n, HEAD_DIM, LANES), F32),
        compiler_params=_cparams("parallel"),
        name="sample_attn",
    )(cols, *caches_t, *small)
    heads = out.transpose(0, 2, 1)
    out_a = heads[:, :A_HEADS].reshape(n, A_HD)
    out_b = heads[:, A_HEADS:A_HEADS + B_HEADS].reshape(n, B_Q)
    return out_a.astype(BF16), out_b.astype(BF16)


def _post_attn_kernel(a_ref, ob_ref, gate_ref, x_ref, mod_ref, wpa_ref, wpb_ref, wo_ref, lng_ref, lnb_ref, o_ref):
    ya = jnp.dot(a_ref[...], wpa_ref[...], preferred_element_type=F32)
    yb = jnp.dot(ob_ref[...], wpb_ref[...], preferred_element_type=F32)
    mix = gate_ref[:, :D_MODEL].astype(F32) * ya + gate_ref[:, D_MODEL:].astype(F32) * yb
    y = jnp.dot(mix.astype(BF16), wo_ref[...], preferred_element_type=F32)
    z = ALPHA * x_ref[...] + mod_ref[2] * y
    o_ref[...] = _layer_norm(z, lng_ref[0:1, :], lnb_ref[0:1, :])


def _post_attn(oa, ob, gates, x, mod, w_pa, w_pb, w_o, ln_g, ln_b, l, tm, rows_per_mod):
    t = x.shape[0]
    row = lambda w: pl.BlockSpec((tm, w), lambda i: (i, 0))
    const = lambda shape: pl.BlockSpec((None,) + shape, lambda i: (l, 0, 0), pipeline_mode=pl.Buffered(1))
    return pl.pallas_call(
        _post_attn_kernel,
        grid=(t // tm,),
        in_specs=[
            row(A_HD), row(B_Q), row(G_COLS), row(D_MODEL),
            _mod_spec(mod, tm, rows_per_mod),
            const((A_HD, D_MODEL)), const((B_Q, D_MODEL)), const((D_MODEL, D_MODEL)),
            const((2, D_MODEL)), const((2, D_MODEL)),
        ],
        out_specs=row(D_MODEL),
        out_shape=jax.ShapeDtypeStruct((t, D_MODEL), F32),
        compiler_params=_cparams("parallel"),
        name="post_attn",
    )(oa, ob, gates, x, mod, w_pa, w_pb, w_o, ln_g, ln_b)


def _ffn_kernel(x_ref, mod_ref, wg_ref, wu_ref, wd_ref, lng_ref, lnb_ref, o_ref, u_ref, acc_ref):
    k = pl.program_id(1)

    @pl.when(k == 0)
    def _():
        u_ref[...] = _modulate(x_ref, mod_ref, 3, 4)
        acc_ref[...] = jnp.zeros_like(acc_ref)

    u = u_ref[...]
    gate = jnp.dot(u, wg_ref[...], preferred_element_type=F32)
    up = jnp.dot(u, wu_ref[...], preferred_element_type=F32)
    act = (gate * jax.nn.sigmoid(gate) * up).astype(BF16)
    acc_ref[...] += jnp.dot(act, wd_ref[...], preferred_element_type=F32)

    @pl.when(k == pl.num_programs(1) - 1)
    def _():
        z = ALPHA * x_ref[...] + mod_ref[5] * acc_ref[...]
        o_ref[...] = _layer_norm(z, lng_ref[1:2, :], lnb_ref[1:2, :])


def _ffn(x, mod, w_gu, w_down, ln_g, ln_b, l, tm, rows_per_mod):
    t = x.shape[0]
    tf = 512
    nf = D_FF // tf
    return pl.pallas_call(
        _ffn_kernel,
        grid=(t // tm, nf),
        in_specs=[
            pl.BlockSpec((tm, D_MODEL), lambda i, k: (i, 0)),
            _mod_spec(mod, tm, rows_per_mod),
            pl.BlockSpec((None, D_MODEL, tf), lambda i, k: (l, 0, k)),
            pl.BlockSpec((None, D_MODEL, tf), lambda i, k: (l, 0, k + nf)),
            pl.BlockSpec((None, tf, D_MODEL), lambda i, k: (l, k, 0)),
            pl.BlockSpec((None, 2, D_MODEL), lambda i, k: (l, 0, 0)),
            pl.BlockSpec((None, 2, D_MODEL), lambda i, k: (l, 0, 0)),
        ],
        out_specs=pl.BlockSpec((tm, D_MODEL), lambda i, k: (i, 0)),
        out_shape=jax.ShapeDtypeStruct((t, D_MODEL), F32),
        scratch_shapes=[pltpu.VMEM((tm, D_MODEL), BF16), pltpu.VMEM((tm, D_MODEL), F32)],
        compiler_params=_cparams("parallel", "arbitrary"),
        name="ffn",
    )(x, mod, w_gu, w_gu, w_down, ln_g, ln_b)


def _t5_bucket(dist):
    exact = N_BUCKETS // 2
    n = jnp.maximum(dist, 0)
    log_ratio = jnp.log(jnp.maximum(n, exact).astype(F32) / exact) / math.log(T5_MAX_DIST / exact)
    large = jnp.minimum(exact + (log_ratio * (N_BUCKETS - exact)).astype(jnp.int32), N_BUCKETS - 1)
    return jnp.where(n < exact, n, large)


def _bias_tables(t5_table):
    tps, tcs, lanes, news = [], [], [], []
    for g, (w, d) in enumerate(DILATIONS):
        bucket = _t5_bucket(d * jnp.arange(BLK + 1))
        onehot = (bucket[:, None] == jnp.arange(N_BUCKETS)[None, :]).astype(F32)
        bias = jnp.dot(onehot, t5_table[:, g * A_HEADS:(g + 1) * A_HEADS],
                       precision=lax.Precision.HIGHEST).T
        wv = jnp.concatenate([bias[:, ::-1], jnp.zeros((A_HEADS, BLK), F32)], axis=1)
        toep = jnp.tile(wv, (1, BLK))[:, :BLK * 2 * BLK].reshape(A_HEADS, BLK, 2 * BLK)
        tps.append(toep[:, :, :BLK])
        tcs.append(toep[:, :, BLK:])
        by_row = jnp.repeat(bias[:, BLK:0:-1], d, axis=1)
        t = jnp.arange(BLK * d)[None, :]
        lanes.append(jnp.where(t % d == 0, by_row, NEG_INF))
        news.append(bias[:, 0])
    new = jnp.concatenate(news + [jnp.zeros((LANES - N_DIL * A_HEADS,), F32)])[None, :]
    return jnp.stack(tps), jnp.stack(tcs), lanes, new


def _rope_tables(pos):
    half = HEAD_DIM // 2
    inv = ROPE_THETA ** (-jnp.arange(half, dtype=F32) / half)
    ang = pos.astype(F32)[:, None] * inv[None]
    cos, sin = jnp.cos(ang), jnp.sin(ang)
    cos_t = jnp.concatenate([cos, cos, cos, cos], axis=-1)
    sin_t = jnp.concatenate([-sin, sin, -sin, sin], axis=-1)
    return cos_t, sin_t


def kernel(x_prompt, x_sample, c_prompt, c_sample, cache_a0, cache_a1, cache_a2, cache_b, t5_table, w_ada, b_ada,
           w_in, sinks, w_pa, w_pb, w_o, w_gu, w_down, ln_g, ln_b):
    tp_rows = BATCH * SEQ
    mod = _adaln(jnp.concatenate([c_prompt, c_sample], axis=0), w_ada, b_ada)
    mod_p = mod[:, :BATCH].reshape(DEPTH, BATCH, 6, 1, D_MODEL).transpose(0, 2, 1, 3, 4)
    mod_s = mod[:, BATCH:].reshape(DEPTH, 1, DEC_BATCH, 6, D_MODEL).transpose(0, 3, 1, 2, 4)

    q_scale = jnp.concatenate([jnp.full((A_HD,), Q_SCALE, F32), jnp.ones((2 * A_HD,), F32)])
    col_scale = jnp.concatenate([jnp.tile(q_scale, N_DIL), jnp.full((B_Q,), Q_SCALE, F32),
                                 jnp.ones((2 * B_KV + G_COLS,), F32)])
    w_a = w_b = w_g = (w_in * col_scale).astype(BF16)
    w_pa_h, w_pb_h, w_o_h = w_pa.astype(BF16), w_pb.astype(BF16), w_o.astype(BF16)
    w_gu_h, w_down_h = w_gu.astype(BF16), w_down.astype(BF16)

    tps, tcs, bias_lanes, bias_new = _bias_tables(t5_table)
    cos_p, sin_p = _rope_tables(jnp.arange(SEQ))
    cos_s, sin_s = _rope_tables(jnp.full((DEC_BATCH,), PAST_LEN))
    caches_t = [c.transpose(0, 1, 3, 4, 5, 2) for c in (cache_a0, cache_a1, cache_a2, cache_b)]

    xp = x_prompt.reshape(tp_rows, D_MODEL)
    xs = x_sample.reshape(DEC_BATCH, D_MODEL)
    kv_rows = None
    rows_a_s = [[] for _ in range(N_DIL)]
    rows_b_p, rows_b_s = [], []
    for l in range(DEPTH):
        sink = sinks[l].reshape(B_HEADS)

        h_a = _proj_a(xp, mod_p[l], w_a, l, ROW_TILE_PROJ, SEQ)
        qb, kvb = _proj_b(xp, mod_p[l], w_b, l, cos_p, sin_p, ROW_TILE_PROJ, SEQ)
        gates = _gates(xp, mod_p[l], w_g, l, ROW_TILE_PROJ, SEQ)
        oa, kv_rows = _dil_attn(h_a, tps, tcs, l, kv_rows)
        ob, b_rows = _band_b(qb, kvb, sink)
        xp = _post_attn(oa, ob, gates, xp, mod_p[l], w_pa_h, w_pb_h, w_o_h, ln_g, ln_b, l, ROW_TILE_POST, SEQ)
        xp = _ffn(xp, mod_p[l], w_gu_h, w_down_h, ln_g, ln_b, l, ROW_TILE_FFN, SEQ)
        rows_b_p.append(b_rows.reshape(BATCH, 2, B_KV_HEADS, HEAD_DIM, BLK))

        h_a = _proj_a(xs, mod_s[l], w_a, l, DEC_BATCH, DEC_BATCH)
        qb, kvb = _proj_b(xs, mod_s[l], w_b, l, cos_s, sin_s, DEC_BATCH, DEC_BATCH)
        gates = _gates(xs, mod_s[l], w_g, l, DEC_BATCH, DEC_BATCH)
        oa, ob = _sample_attn(h_a, qb, kvb, caches_t, l, bias_lanes, bias_new, sink)
        xs = _post_attn(oa, ob, gates, xs, mod_s[l], w_pa_h, w_pb_h, w_o_h, ln_g, ln_b, l, DEC_BATCH, DEC_BATCH)
        xs = _ffn(xs, mod_s[l], w_gu_h, w_down_h, ln_g, ln_b, l, DEC_BATCH, DEC_BATCH)
        ha = h_a.reshape(DEC_BATCH, 1, N_DIL, 3, A_HEADS, HEAD_DIM)
        for g in range(N_DIL):
            rows_a_s[g].append(ha[:, :, g, 1:])
        rows_b_s.append(kvb.reshape(DEC_BATCH, 1, 2, B_KV_HEADS, HEAD_DIM))

    to_rows_major = lambda a: a.transpose(0, 1, 5, 2, 3, 4)
    outs = [xp.reshape(BATCH, SEQ, D_MODEL), xs.reshape(DEC_BATCH, 1, D_MODEL)]
    for g in range(N_DIL):
        outs.append(to_rows_major(kv_rows[g]))
        outs.append(jnp.stack(rows_a_s[g]))
    outs.append(to_rows_major(jnp.stack(rows_b_p)))
    outs.append(jnp.stack(rows_b_s))
    return tuple(outs)
```
